```python
import jax, jax.numpy as jnp
from jax import lax
import numpy as np

D_MODEL = 1024
BATCH = 8
SEQ = 8192
DEPTH = 2

N_MIXERS = 2
N_RWKV = (DEPTH + 1) // 2
N_SGU = DEPTH // 2
RWKV_HEAD_DIM = 64
RWKV_HEADS = D_MODEL // RWKV_HEAD_DIM
DECAY_LORA = 64
AAA_LORA = 64
GATE_LORA = 128
RWKV_GN_EPS = 64e-5
CHUNK = 128
SGU_WIDTH = D_MODEL
SGU_GROUPS = 16
SGU_GROUP_DIM = SGU_WIDTH // SGU_GROUPS
SGU_LN_EPS = 1e-5
MEM_TOKENS = 256
XATTN_HEADS = 4
XATTN_HEAD_DIM = D_MODEL // XATTN_HEADS
FFN_WIDTH = 2816
CONV_WIDTH = 3
RMS_EPS = 1e-6

kernel_name = "rwkv7_sgu_interleaved_hybrid"


def rmsnorm(x, g):
    x32 = x.astype(jnp.float32)
    y = x32 * lax.rsqrt(jnp.mean(x32 * x32, axis=-1, keepdims=True) + RMS_EPS)
    return (y * g.astype(jnp.float32)).astype(x.dtype)


def wkv7_scan(r, w, k, v, kk, a):
    bsz, _, nh, nd = r.shape
    seq_first = [jnp.moveaxis(t.astype(jnp.float32), 1, 0) for t in (r, w, k, v, kk, a)]

    def step(state, inp):
        r_t, w_t, k_t, v_t, kk_t, a_t = inp
        sa = jnp.einsum('bhvk,bhk->bhv', state, kk_t)
        state = (state * w_t[:, :, None, :]
                 - sa[..., None] * (kk_t * a_t)[:, :, None, :]
                 + v_t[..., None] * k_t[:, :, None, :])
        return state, jnp.einsum('bhvk,bhk->bhv', state, r_t)

    init = jnp.zeros((bsz, nh, nd, nd), jnp.float32)
    _, out = lax.scan(step, init, tuple(seq_first))
    return jnp.moveaxis(out, 0, 1)


def rwkv7_time_mix(h, mu, w_rkv, w0, w1, w2, a0, a1, a2, g1, g2, k_k, k_a, r_k, lnx_g, lnx_b, w_o):
    bsz, seq, dm = h.shape
    heads = lambda t: t.reshape(bsz, seq, RWKV_HEADS, RWKV_HEAD_DIM)
    dx = jnp.pad(h, ((0, 0), (1, 0), (0, 0)))[:, :-1] - h
    xr, xw, xk, xv, xa, xg = [h + dx * mu[j] for j in range(6)]
    r = xr @ w_rkv[0]
    k = xk @ w_rkv[1]
    v = xv @ w_rkv[2]
    w_log = -jax.nn.softplus(-(w0 + jnp.tanh(xw @ w1) @ w2)) - 0.5
    decay = jnp.exp(-jnp.exp(w_log.astype(jnp.float32)))
    a = jax.nn.sigmoid(a0 + (xa @ a1) @ a2)
    g = jax.nn.sigmoid(xg @ g1) @ g2
    kk32 = heads(k * k_k).astype(jnp.float32)
    kk = kk32 / jnp.maximum(jnp.linalg.norm(kk32, axis=-1, keepdims=True), 1e-12)
    k = k * (1 + (a - 1) * k_a)
    o = wkv7_scan(heads(r), heads(decay), heads(k), heads(v), kk, heads(a))
    mean = jnp.mean(o, axis=-1, keepdims=True)
    var = jnp.mean(jnp.square(o - mean), axis=-1, keepdims=True)
    o = ((o - mean) * lax.rsqrt(var + RWKV_GN_EPS)).reshape(bsz, seq, dm)
    o = o * lnx_g.astype(jnp.float32) + lnx_b.astype(jnp.float32)
    bonus = jnp.sum(heads(r) * heads(k) * r_k, axis=-1, keepdims=True) * heads(v)
    o = o + bonus.reshape(bsz, seq, dm)
    return (o * g).astype(h.dtype) @ w_o


def chunked_sgu_mixer(h, w_in, ln_g, ln_b, w_s, b_s, w_out):
    bsz, seq, _ = h.shape
    z = jax.nn.gelu(h @ w_in)
    u, v = jnp.split(z, 2, axis=-1)
    v32 = v.astype(jnp.float32)
    mean = jnp.mean(v32, axis=-1, keepdims=True)
    var = jnp.mean(jnp.square(v32 - mean), axis=-1, keepdims=True)
    v = ((v32 - mean) * lax.rsqrt(var + SGU_LN_EPS) * ln_g + ln_b).astype(h.dtype)
    v = v.reshape(bsz, seq // CHUNK, CHUNK, SGU_GROUPS, SGU_GROUP_DIM)
    causal = jnp.tril(jnp.ones((CHUNK, CHUNK), dtype=bool))
    w_causal = jnp.where(causal[None], w_s, jnp.zeros_like(w_s))
    mixed = jnp.einsum('gts,bcsgd->bctgd', w_causal, v) + b_s.T[:, :, None]
    return (u * mixed.reshape(bsz, seq, SGU_WIDTH)) @ w_out


def memory_cross_attention(h, mem_n, w_q, w_kv, w_o):
    bsz, seq, dm = h.shape
    q = (h @ w_q).reshape(bsz, seq, XATTN_HEADS, XATTN_HEAD_DIM)
    k, v = jnp.split(mem_n @ w_kv, 2, axis=-1)
    k = k.reshape(bsz, -1, XATTN_HEADS, XATTN_HEAD_DIM)
    v = v.reshape(bsz, -1, XATTN_HEADS, XATTN_HEAD_DIM)
    s = jnp.einsum('bshd,bmhd->bhsm', q, k).astype(jnp.float32) * (XATTN_HEAD_DIM ** -0.5)
    p = jax.nn.softmax(s, axis=-1).astype(h.dtype)
    o = jnp.einsum('bhsm,bmhd->bshd', p, v).reshape(bsz, seq, dm)
    return o @ w_o


def conv_glu_ffn(h, w_up, conv_w, conv_b, w_down):
    seq = h.shape[1]
    gate, val = jnp.split(h @ w_up, 2, axis=-1)
    gp = jnp.pad(gate, ((0, 0), (CONV_WIDTH - 1, 0), (0, 0)))
    conv = sum(gp[:, j:j + seq] * conv_w[j] for j in range(CONV_WIDTH)) + conv_b
    return (jax.nn.silu(conv) * val) @ w_down


def setup_inputs(seed: int = 0) -> dict:
    key = jax.random.key(seed)
    ks = iter(jax.random.split(key, 48))
    D = D_MODEL

    def nrm(shape, scale):
        return jax.random.normal(next(ks), shape, jnp.float32) * scale

    def gain(shape):
        return 1.0 + nrm(shape, 0.02)

    return {
        "x": nrm((BATCH, SEQ, D), 1.0),
        "mem": nrm((BATCH, MEM_TOKENS, D), 1.0),
        "norm_mix": gain((DEPTH, D)),
        "norm_mem": gain((DEPTH, D)),
        "norm_ffn": gain((DEPTH, D)),
        "norm_final": gain((D,)),
        "mem_norm": gain((D,)),
        "rw_mu": jax.random.uniform(next(ks), (N_RWKV, 6, D), jnp.float32),
        "rw_w_rkv": nrm((N_RWKV, 3, D, D), D ** -0.5),
        "rw_w0": jax.random.uniform(next(ks), (N_RWKV, D), jnp.float32, -6.0, -1.0),
        "rw_w1": nrm((N_RWKV, D, DECAY_LORA), D ** -0.5),
        "rw_w2": nrm((N_RWKV, DECAY_LORA, D), 0.5 * DECAY_LORA ** -0.5),
        "rw_a0": nrm((N_RWKV, D), 0.1),
        "rw_a1": nrm((N_RWKV, D, AAA_LORA), D ** -0.5),
        "rw_a2": nrm((N_RWKV, AAA_LORA, D), 0.5 * AAA_LORA ** -0.5),
        "rw_g1": nrm((N_RWKV, D, GATE_LORA), D ** -0.5),
        "rw_g2": nrm((N_RWKV, GATE_LORA, D), GATE_LORA ** -0.5),
        "rw_k_k": 0.85 + nrm((N_RWKV, D), 0.05),
        "rw_k_a": 1.0 + nrm((N_RWKV, D), 0.05),
        "rw_r_k": nrm((N_RWKV, RWKV_HEADS, RWKV_HEAD_DIM), 0.1),
        "rw_lnx_g": gain((N_RWKV, D)),
        "rw_lnx_b": nrm((N_RWKV, D), 0.02),
        "rw_w_o": nrm((N_RWKV, D, D), D ** -0.5),
        "gm_w_in": nrm((N_SGU, D, 2 * SGU_WIDTH), D ** -0.5),
        "gm_ln_g": gain((N_SGU, SGU_WIDTH)),
        "gm_ln_b": nrm((N_SGU, SGU_WIDTH), 0.02),
        "gm_w_s": nrm((N_SGU, SGU_GROUPS, CHUNK, CHUNK), CHUNK ** -0.5),
        "gm_b_s": 1.0 + nrm((N_SGU, SGU_GROUPS, CHUNK), 0.1),
        "gm_w_out": nrm((N_SGU, SGU_WIDTH, D), SGU_WIDTH ** -0.5),
        "ca_w_q": nrm((DEPTH, D, D), D ** -0.5),
        "ca_w_kv": nrm((DEPTH, D, 2 * D), D ** -0.5),
        "ca_w_o": nrm((DEPTH, D, D), D ** -0.5),
        "ff_w_up": nrm((DEPTH, D, 2 * FFN_WIDTH), D ** -0.5),
        "ff_conv_w": nrm((DEPTH, CONV_WIDTH, FFN_WIDTH), CONV_WIDTH ** -0.5),
        "ff_conv_b": nrm((DEPTH, FFN_WIDTH), 0.02),
        "ff_w_down": nrm((DEPTH, FFN_WIDTH, D), FFN_WIDTH ** -0.5),
    }


def reference(x, mem, norm_mix, norm_mem, norm_ffn, norm_final, mem_norm,
              rw_mu, rw_w_rkv, rw_w0, rw_w1, rw_w2, rw_a0, rw_a1, rw_a2, rw_g1, rw_g2,
              rw_k_k, rw_k_a, rw_r_k, rw_lnx_g, rw_lnx_b, rw_w_o,
              gm_w_in, gm_ln_g, gm_ln_b, gm_w_s, gm_b_s, gm_w_out,
              ca_w_q, ca_w_kv, ca_w_o, ff_w_up, ff_conv_w, ff_conv_b, ff_w_down):
    mem_n = rmsnorm(mem, mem_norm)
    for i in range(DEPTH):
        h = rmsnorm(x, norm_mix[i])
        j = i // N_MIXERS
        if i % N_MIXERS == 0:
            x = x + rwkv7_time_mix(h, rw_mu[j], rw_w_rkv[j], rw_w0[j], rw_w1[j], rw_w2[j],
                                   rw_a0[j], rw_a1[j], rw_a2[j], rw_g1[j], rw_g2[j],
                                   rw_k_k[j], rw_k_a[j], rw_r_k[j], rw_lnx_g[j], rw_lnx_b[j],
                                   rw_w_o[j])
        else:
            x = x + chunked_sgu_mixer(h, gm_w_in[j], gm_ln_g[j], gm_ln_b[j], gm_w_s[j],
                                      gm_b_s[j], gm_w_out[j])
        x = x + memory_cross_attention(rmsnorm(x, norm_mem[i]), mem_n,
                                       ca_w_q[i], ca_w_kv[i], ca_w_o[i])
        x = x + conv_glu_ffn(rmsnorm(x, norm_ffn[i]), ff_w_up[i], ff_conv_w[i],
                             ff_conv_b[i], ff_w_down[i])
    return rmsnorm(x, norm_final)
```

```python
import functools

import jax
import jax.numpy as jnp
from jax import lax
from jax.experimental import pallas as pl
from jax.experimental.pallas import tpu as pltpu

F32 = jnp.float32
BF16 = jnp.bfloat16

RWKV_HEAD_DIM = 64
RWKV_GN_EPS = 64e-5
SGU_CHUNK = 128
SGU_GROUPS = 16
SGU_LN_EPS = 1e-5
XATTN_HEADS = 4
RMS_EPS = 1e-6

MXU_TILE = 256
SUBLANES = 8
WKV_CHUNK = 64
WKV_HEADS_PER_BLOCK = MXU_TILE // RWKV_HEAD_DIM
VMEM_LIMIT = 56 * 1024 * 1024


def _cparams(sem):
    return pltpu.CompilerParams(dimension_semantics=sem, vmem_limit_bytes=VMEM_LIMIT)


def _dot(a, b):
    return jnp.dot(a.astype(BF16), b.astype(BF16), preferred_element_type=F32)


def _dot_nt(a, b):
    return lax.dot_general(a.astype(BF16), b.astype(BF16), (((1,), (1,)), ((), ())),
                           preferred_element_type=F32)


def _dot_tn(a, b):
    return lax.dot_general(a.astype(BF16), b.astype(BF16), (((0,), (0,)), ((), ())),
                           preferred_element_type=F32)


def _split3(x):
    hi = x.astype(BF16)
    r1 = x - hi.astype(F32)
    mid = r1.astype(BF16)
    lo = (r1 - mid.astype(F32)).astype(BF16)
    return hi, mid, lo


def _dot_exact_lhs(a_bf16, x):
    return sum(jnp.dot(a_bf16, t, preferred_element_type=F32) for t in _split3(x))


def _dot_exact_rhs(x, b_bf16):
    return sum(jnp.dot(t, b_bf16, preferred_element_type=F32) for t in _split3(x))


def _rms(x, g):
    return x * lax.rsqrt(jnp.mean(x * x, axis=-1, keepdims=True) + RMS_EPS) * g


def _group_sum(x, ones_bd):
    d = x.shape[-1]
    parts = [_dot(x[:, j:j + MXU_TILE], ones_bd) for j in range(0, d, MXU_TILE)]
    return jnp.concatenate(parts, axis=-1)


def _iota(shape, dim):
    return lax.broadcasted_iota(jnp.int32, shape, dim)


def _mem_kv_kernel(mem_ref, g_ref, w_ref, o_ref):
    mn = _rms(mem_ref[...], g_ref[...])
    o_ref[...] = _dot(mn, w_ref[...]).astype(o_ref.dtype)


def _mem_kv(mem, mem_norm, w_kv):
    b, m, d = mem.shape
    nl = w_kv.shape[0]
    return pl.pallas_call(
        _mem_kv_kernel,
        grid=(nl, b),
        in_specs=[pl.BlockSpec((None, m, d), lambda l, i: (i, 0, 0)),
                  pl.BlockSpec((1, d), lambda l, i: (0, 0)),
                  pl.BlockSpec((None, d, 2 * d), lambda l, i: (l, 0, 0))],
        out_specs=pl.BlockSpec((None, None, m, 2 * d), lambda l, i: (l, i, 0, 0)),
        out_shape=jax.ShapeDtypeStruct((nl, b, m, 2 * d), BF16),
        compiler_params=_cparams(("arbitrary", "arbitrary")),
        name="mem_kv",
    )(mem, mem_norm.reshape(1, d), w_kv.astype(BF16))


def _rwkv_proj_kernel(x_ref, xp_ref, vec_ref, ones_ref, wrkv_ref, w1_ref, w2_ref, a1_ref, a2_ref,
                      g1_ref, g2_ref,
                      r_ref, lw_ref, k_ref, v_ref, kk_ref, b_ref, g_ref, bonus_ref):
    i = pl.program_id(1)
    vec = vec_ref[...]
    row = lambda n: vec[n:n + 1, :]
    gn, w0, a0, k_k, k_a, r_k = row(0), row(7), row(8), row(9), row(10), row(11)
    h = _rms(x_ref[...], gn)
    hp = _rms(xp_ref[...], gn)[SUBLANES - 1:SUBLANES, :]
    hp = jnp.where(i == 0, 0.0, hp)
    rows = _iota(h.shape, 0)
    h_prev = jnp.where(rows == 0, hp, pltpu.roll(h, 1, axis=0))
    dx = h_prev - h
    mix = lambda j: (h + dx * row(1 + j)).astype(BF16)
    r = _dot(mix(0), wrkv_ref[0])
    k = _dot(mix(2), wrkv_ref[1])
    v = _dot(mix(3), wrkv_ref[2])
    wl = _dot(jnp.tanh(_dot(mix(1), w1_ref[...])), w2_ref[...])
    w_log = -jax.nn.softplus(-(w0 + wl)) - 0.5
    lw = -jnp.exp(w_log)
    a = jax.nn.sigmoid(a0 + _dot(_dot(mix(4), a1_ref[...]), a2_ref[...]))
    g = _dot(jax.nn.sigmoid(_dot(mix(5), g1_ref[...])), g2_ref[...])
    ones_bd = ones_ref[...]
    kk_raw = k * k_k
    kk_norm = jnp.sqrt(_group_sum(kk_raw * kk_raw, ones_bd))
    kk = kk_raw / jnp.maximum(kk_norm, 1e-12)
    k2 = k * (1.0 + (a - 1.0) * k_a)
    bonus = _group_sum(r * k2 * r_k, ones_bd) * v
    r_ref[...] = r.astype(r_ref.dtype)
    lw_ref[...] = lw
    k_ref[...] = k2.astype(k_ref.dtype)
    v_ref[...] = v.astype(v_ref.dtype)
    kk_ref[...] = kk.astype(kk_ref.dtype)
    b_ref[...] = (kk * a).astype(b_ref.dtype)
    g_ref[...] = g.astype(g_ref.dtype)
    bonus_ref[...] = bonus.astype(bonus_ref.dtype)


def _rwkv_proj(x, vecs, ones_bd, w_rkv, w1, w2, a1, a2, g1, g2, tm):
    b, s, d = x.shape
    tm = min(tm, s)
    tok = pl.BlockSpec((None, tm, d), lambda bi, i: (bi, i, 0))
    full = lambda arr: pl.BlockSpec(arr.shape, lambda bi, i: (0,) * arr.ndim)
    prev = pl.BlockSpec((None, SUBLANES, d),
                        lambda bi, i: (bi, jnp.maximum(i * (tm // SUBLANES) - 1, 0), 0))
    weights = (vecs, ones_bd, w_rkv, w1, w2, a1, a2, g1, g2)
    out_dt = (BF16, F32, BF16, BF16, BF16, BF16, BF16, BF16)
    return pl.pallas_call(
        _rwkv_proj_kernel,
        grid=(b, s // tm),
        in_specs=[tok, prev] + [full(w) for w in weights],
        out_specs=[tok] * 8,
        out_shape=[jax.ShapeDtypeStruct((b, s, d), dt) for dt in out_dt],
        compiler_params=_cparams(("arbitrary", "arbitrary")),
        name="rwkv_proj",
    )(x, x, *weights)


def _wkv_kernel(r_ref, lw_ref, k_ref, v_ref, kk_ref, b_ref, tri_ref, o_ref, st_ref,
                q_sc, ol_sc, g_sc, h_sc, *, n_chunks):
    L, NH, W = WKV_CHUNK, WKV_HEADS_PER_BLOCK, MXU_TILE

    @pl.when(pl.program_id(2) == 0)
    def _():
        st_ref[...] = jnp.zeros_like(st_ref)

    ri = _iota((W, W), 0)
    ci = _iota((W, W), 1)
    strict = ri > ci
    incl = ri >= ci
    diag = ri == ci
    lane_head = _iota((L, W), 1) // RWKV_HEAD_DIM
    tri = tri_ref[...]

    def stack(x):
        return jnp.concatenate([jnp.where(lane_head == h, x, 0.0) for h in range(NH)],
                               axis=0).astype(BF16)

    for c in range(n_chunks):
        sl = pl.ds(c * L, L)
        lw = lw_ref[sl, :]
        r = r_ref[sl, :].astype(F32)
        k = k_ref[sl, :].astype(F32)
        v = v_ref[sl, :].astype(F32)
        kk = kk_ref[sl, :].astype(F32)
        b = b_ref[sl, :].astype(F32)
        cum = _dot_exact_lhs(tri, lw)
        cum_last = cum[L - 1:L, :]
        w_in = jnp.exp(cum)
        w_ex = jnp.exp(cum - lw)
        w_inv = jnp.exp(-cum)
        w_tail = jnp.exp(cum_last - cum)
        rt = stack(r * w_in)
        at = stack(-kk * w_ex)
        bt = stack(b * w_inv)
        kt = stack(k * w_inv)
        bh = stack(b * w_tail)
        kh = stack(k * w_tail)
        vs = stack(v)
        bk = jnp.concatenate([bt, kt], axis=0)
        a_all = _dot_nt(at, bk)
        r_all = _dot_nt(rt, bk)
        a_ab = jnp.where(strict, a_all[:, :W], 0.0)
        a_ak = jnp.where(strict, a_all[:, W:], 0.0).astype(BF16)
        a_rb = jnp.where(incl, r_all[:, :W], 0.0).astype(BF16)
        a_rk = jnp.where(incl, r_all[:, W:], 0.0).astype(BF16)
        t = jnp.where(diag, 1.0, a_ab)
        p2 = a_ab
        for _ in range(5):
            p2b = p2.astype(BF16)
            p2 = _dot(p2b, p2b)
            t = t + _dot(t, p2)
        tb = t.astype(BF16)
        av = _dot(a_ak, vs)
        pu = _dot(tb, jnp.concatenate([at, av.astype(BF16)], axis=1))
        pub = pu.astype(BF16)
        qo = _dot(a_rb, pub)
        q_sc[c] = (rt.astype(F32) + qo[:, :W]).astype(BF16)
        ol_sc[c] = qo[:, W:] + _dot(a_rk, vs)
        gh = _dot_tn(bh, pub)
        g_sc[c] = (jnp.where(diag, jnp.exp(cum_last), 0.0) + gh[:, :W]).astype(BF16)
        h_sc[c] = gh[:, W:] + _dot_tn(kh, vs)

    st = st_ref[...]
    for c in range(n_chunks):
        stb = st.astype(BF16)
        o_s = jnp.dot(q_sc[c], stb, preferred_element_type=F32) + ol_sc[c]
        o = o_s[0:L]
        for h in range(1, NH):
            o = o + o_s[h * L:(h + 1) * L]
        o_ref[pl.ds(c * L, L), :] = o.astype(o_ref.dtype)
        st = jnp.dot(g_sc[c], stb, preferred_element_type=F32) + h_sc[c]
    st_ref[...] = st


def _wkv(r, lw, k, v, kk, b, tc):
    bsz, s, d = r.shape
    tc = min(tc, s)
    n_chunks = tc // WKV_CHUNK
    W = MXU_TILE
    tri = (jnp.arange(WKV_CHUNK)[:, None] >= jnp.arange(WKV_CHUNK)[None, :]).astype(BF16)
    blk = pl.BlockSpec((None, tc, W), lambda bi, q, i: (bi, i, q))
    return pl.pallas_call(
        functools.partial(_wkv_kernel, n_chunks=n_chunks),
        grid=(bsz, d // W, s // tc),
        in_specs=[blk] * 6 + [pl.BlockSpec(tri.shape, lambda bi, q, i: (0, 0))],
        out_specs=blk,
        out_shape=jax.ShapeDtypeStruct((bsz, s, d), BF16),
        scratch_shapes=[pltpu.VMEM((W, W), F32),
                        pltpu.VMEM((n_chunks, W, W), BF16),
                        pltpu.VMEM((n_chunks, W, W), F32),
                        pltpu.VMEM((n_chunks, W, W), BF16),
                        pltpu.VMEM((n_chunks, W, W), F32)],
        compiler_params=_cparams(("arbitrary", "arbitrary", "arbitrary")),
        name="wkv",
    )(r, lw, k, v, kk, b, tri)


def _rwkv_out_kernel(x_ref, o_ref, g_ref, bonus_ref, vec_ref, ones_ref, wo_ref, out_ref):
    vec = vec_ref[...]
    lnx_g, lnx_b = vec[0:1, :], vec[1:2, :]
    ones_bd = ones_ref[...]
    o = o_ref[...].astype(F32)
    inv_n = 1.0 / RWKV_HEAD_DIM
    mean = _group_sum(o, ones_bd) * inv_n
    dlt = o - mean
    var = _group_sum(dlt * dlt, ones_bd) * inv_n
    y = dlt * lax.rsqrt(var + RWKV_GN_EPS) * lnx_g + lnx_b + bonus_ref[...].astype(F32)
    y = y * g_ref[...].astype(F32)
    out_ref[...] = x_ref[...] + _dot(y, wo_ref[...])


def _rwkv_out(x, o, g, bonus, vecs, ones_bd, w_o, tm):
    b, s, d = x.shape
    tm = min(tm, s)
    tok = pl.BlockSpec((None, tm, d), lambda bi, i: (bi, i, 0))
    full = lambda arr: pl.BlockSpec(arr.shape, lambda bi, i: (0,) * arr.ndim)
    return pl.pallas_call(
        _rwkv_out_kernel,
        grid=(b, s // tm),
        in_specs=[tok, tok, tok, tok, full(vecs), full(ones_bd), full(w_o)],
        out_specs=tok,
        out_shape=jax.ShapeDtypeStruct((b, s, d), F32),
        compiler_params=_cparams(("arbitrary", "arbitrary")),
        name="rwkv_out",
    )(x, o, g, bonus, vecs, ones_bd, w_o)


def _xattn_kernel(x_ref, g_ref, wq_ref, kv_ref, wo_ref, out_ref):
    x = x_ref[...]
    d = x.shape[-1]
    hd = d // XATTN_HEADS
    q = _dot(_rms(x, g_ref[...]), wq_ref[...])
    outs = []
    for h in range(XATTN_HEADS):
        kh = kv_ref[:, h * hd:(h + 1) * hd]
        vh = kv_ref[:, d + h * hd:d + (h + 1) * hd]
        s = _dot_nt(q[:, h * hd:(h + 1) * hd], kh) * (hd ** -0.5)
        p = jnp.exp(s - jnp.max(s, axis=-1, keepdims=True))
        p = p / jnp.sum(p, axis=-1, keepdims=True)
        outs.append(_dot(p, vh).astype(BF16))
    out_ref[...] = x + _dot(jnp.concatenate(outs, axis=-1), wo_ref[...])


def _xattn(x, g, w_q, kv, w_o, tm):
    b, s, d = x.shape
    tm = min(tm, s)
    m = kv.shape[1]
    tok = pl.BlockSpec((None, tm, d), lambda bi, i: (bi, i, 0))
    full = lambda arr: pl.BlockSpec(arr.shape, lambda bi, i: (0,) * arr.ndim)
    return pl.pallas_call(
        _xattn_kernel,
        grid=(b, s // tm),
        in_specs=[tok, full(g), full(w_q),
                  pl.BlockSpec((None, m, 2 * d), lambda bi, i: (bi, 0, 0)), full(w_o)],
        out_specs=tok,
        out_shape=jax.ShapeDtypeStruct((b, s, d), F32),
        compiler_params=_cparams(("arbitrary", "arbitrary")),
        name="xattn",
    )(x, g, w_q, kv, w_o)


def _ffn_kernel(x_ref, g_ref, wg_ref, wv_ref, cw_ref, cb_ref, wd_ref, gf_ref, out_ref,
                xn_sc, carry_sc, *, tiles_per_seq, final_norm):
    t = pl.program_id(0)
    j = pl.program_id(1)
    tm = x_ref.shape[0]

    @pl.when(j == 0)
    def _():
        x = x_ref[...]
        xn_sc[...] = _rms(x, g_ref[...]).astype(BF16)
        out_ref[...] = x

    xn = xn_sc[...]
    gate = jnp.dot(xn, wg_ref[...], preferred_element_type=F32)
    val = jnp.dot(xn, wv_ref[...], preferred_element_type=F32)
    @pl.when((t % tiles_per_seq) == 0)
    def _():
        carry_sc[j] = jnp.zeros(carry_sc.shape[1:], F32)

    prev = carry_sc[j]
    carry_sc[j] = gate[tm - SUBLANES:tm, :]
    rows = _iota(gate.shape, 0)
    s1 = jnp.where(rows == 0, prev[SUBLANES - 1:SUBLANES, :], pltpu.roll(gate, 1, axis=0))
    s2 = pltpu.roll(gate, 2, axis=0)
    s2 = jnp.where(rows == 0, prev[SUBLANES - 2:SUBLANES - 1, :], s2)
    s2 = jnp.where(rows == 1, prev[SUBLANES - 1:SUBLANES, :], s2)
    cw = cw_ref[...]
    conv = s2 * cw[0:1, :] + s1 * cw[1:2, :] + gate * cw[2:3, :] + cb_ref[...]
    act = jax.nn.silu(conv) * val
    out_ref[...] += _dot(act, wd_ref[...])

    if final_norm:
        @pl.when(j == pl.num_programs(1) - 1)
        def _():
            out_ref[...] = _rms(out_ref[...], gf_ref[...])


def _ffn(x, g, w_up, conv_w, conv_b, w_down, g_final, final_norm, tm):
    b, s, d = x.shape
    f = w_down.shape[0]
    tm = min(tm, s)
    fc = MXU_TILE
    nj = f // fc
    xf = x.reshape(b * s, d)
    tok = pl.BlockSpec((tm, d), lambda t, j: (t, 0))
    vec = pl.BlockSpec((1, d), lambda t, j: (0, 0))
    out = pl.pallas_call(
        functools.partial(_ffn_kernel, tiles_per_seq=s // tm, final_norm=final_norm),
        grid=(b * s // tm, nj),
        in_specs=[tok, vec,
                  pl.BlockSpec((d, fc), lambda t, j: (0, j)),
                  pl.BlockSpec((d, fc), lambda t, j: (0, nj + j)),
                  pl.BlockSpec((conv_w.shape[0], fc), lambda t, j: (0, j)),
                  pl.BlockSpec((1, fc), lambda t, j: (0, j)),
                  pl.BlockSpec((fc, d), lambda t, j: (j, 0)),
                  vec],
        out_specs=tok,
        out_shape=jax.ShapeDtypeStruct((b * s, d), F32),
        scratch_shapes=[pltpu.VMEM((tm, d), BF16), pltpu.VMEM((nj, SUBLANES, fc), F32)],
        compiler_params=_cparams(("arbitrary", "arbitrary")),
        name="ffn",
    )(xf, g, w_up, w_up, conv_w, conv_b, w_down, g_final)
    return out.reshape(b, s, d)


def _sgu_kernel(x_ref, vec_ref, win_ref, ws_ref, bst_ref, expand_ref, wout_ref, out_ref,
                wc_sc, bias_sc):
    d = x_ref.shape[-1]
    tm = x_ref.shape[0]
    C, G = SGU_CHUNK, SGU_GROUPS
    gd = d // G
    gpb = MXU_TILE // gd

    @pl.when((pl.program_id(0) == 0) & (pl.program_id(1) == 0))
    def _():
        causal = _iota((C, C), 0) >= _iota((C, C), 1)
        for g in range(G):
            wc_sc[g] = jnp.where(causal, ws_ref[g], 0.0).astype(BF16)
        bias_sc[...] = _dot_exact_rhs(bst_ref[...], expand_ref[...])

    vec = vec_ref[...]
    gn, ln_g, ln_b = vec[0:1, :], vec[1:2, :], vec[2:3, :]
    x = x_ref[...]
    z = jax.nn.gelu(_dot(_rms(x, gn), win_ref[...]))
    u = z[:, :d]
    v = z[:, d:]
    mean = jnp.mean(v, axis=-1, keepdims=True)
    dv = v - mean
    var = jnp.mean(dv * dv, axis=-1, keepdims=True)
    vn = (dv * lax.rsqrt(var + SGU_LN_EPS) * ln_g + ln_b).astype(BF16)
    lane_grp = _iota((C, MXU_TILE), 1) // gd
    bias = bias_sc[...]
    chunks = []
    for c in range(tm // C):
        blocks = []
        for blk in range(d // MXU_TILE):
            vb = vn[c * C:(c + 1) * C, blk * MXU_TILE:(blk + 1) * MXU_TILE]
            acc = None
            for gi in range(gpb):
                m = jnp.dot(wc_sc[blk * gpb + gi], vb, preferred_element_type=F32)
                m = jnp.where(lane_grp == gi, m, 0.0)
                acc = m if acc is None else acc + m
            blocks.append(acc)
        chunks.append(jnp.concatenate(blocks, axis=-1) + bias)
    mixed = jnp.concatenate(chunks, axis=0)
    out_ref[...] = x + _dot(u * mixed, wout_ref[...])


def _sgu(x, vecs, w_in, w_s, b_s, w_out, tm):
    b, s, d = x.shape
    tm = min(tm, s)
    gd = d // SGU_GROUPS
    expand = (jnp.arange(d)[None, :] // gd == jnp.arange(SGU_CHUNK)[:, None]).astype(BF16)
    bst = jnp.pad(b_s.T, ((0, 0), (0, SGU_CHUNK - SGU_GROUPS)))
    tok = pl.BlockSpec((None, tm, d), lambda bi, i: (bi, i, 0))
    full = lambda arr: pl.BlockSpec(arr.shape, lambda bi, i: (0,) * arr.ndim)
    ins = (vecs, w_in, w_s, bst, expand, w_out)
    return pl.pallas_call(
        _sgu_kernel,
        grid=(b, s // tm),
        in_specs=[tok] + [full(a) for a in ins],
        out_specs=tok,
        out_shape=jax.ShapeDtypeStruct((b, s, d), F32),
        scratch_shapes=[pltpu.VMEM((SGU_GROUPS, SGU_CHUNK, SGU_CHUNK), BF16),
                        pltpu.VMEM((SGU_CHUNK, d), F32)],
        compiler_params=_cparams(("arbitrary", "arbitrary")),
        name="sgu",
    )(x, *ins)


def _pad_rows(rows, d):
    n = -(-len(rows) // SUBLANES) * SUBLANES
    arr = jnp.stack([r.reshape(d).astype(F32) for r in rows])
    return jnp.pad(arr, ((0, n - len(rows)), (0, 0)))


def kernel(x, mem, norm_mix, norm_mem, norm_ffn, norm_final, mem_norm, rw_mu, rw_w_rkv, rw_w0, rw_w1, rw_w2, rw_a0, rw_a1, rw_a2, rw_g1, rw_g2, rw_k_k, rw_k_a, rw_r_k, rw_lnx_g, rw_lnx_b, rw_w_o, gm_w_in, gm_ln_g, gm_ln_b, gm_w_s, gm_b_s, gm_w_out, ca_w_q, ca_w_kv, ca_w_o, ff_w_up, ff_conv_w, ff_conv_b, ff_w_down):
    d = x.shape[-1]
    depth = norm_mix.shape[0]
    bf = lambda w: w.astype(BF16)
    lane = jnp.arange(MXU_TILE)
    ones_bd = (lane[:, None] // RWKV_HEAD_DIM == lane[None, :] // RWKV_HEAD_DIM).astype(BF16)
    kv = _mem_kv(mem, mem_norm, ca_w_kv)
    for i in range(depth):
        j = i // 2
        if i % 2 == 0:
            vecs = _pad_rows([norm_mix[i]] + [rw_mu[j, n] for n in range(6)]
                             + [rw_w0[j], rw_a0[j], rw_k_k[j], rw_k_a[j], rw_r_k[j]], d)
            r, lw, k, v, kk, b, g, bonus = _rwkv_proj(
                x, vecs, ones_bd, bf(rw_w_rkv[j]), bf(rw_w1[j]), bf(rw_w2[j]), bf(rw_a1[j]),
                bf(rw_a2[j]), bf(rw_g1[j]), bf(rw_g2[j]), tm=256)
            o = _wkv(r, lw, k, v, kk, b, tc=512)
            x = _rwkv_out(x, o, g, bonus, _pad_rows([rw_lnx_g[j], rw_lnx_b[j]], d), ones_bd,
                          bf(rw_w_o[j]), tm=512)
        else:
            x = _sgu(x, _pad_rows([norm_mix[i], gm_ln_g[j], gm_ln_b[j]], d), bf(gm_w_in[j]),
                     gm_w_s[j], gm_b_s[j], bf(gm_w_out[j]), tm=512)
        x = _xattn(x, norm_mem[i].reshape(1, d), bf(ca_w_q[i]), kv[i], bf(ca_w_o[i]), tm=512)
        x = _ffn(x, norm_ffn[i].reshape(1, d), bf(ff_w_up[i]), ff_conv_w[i],
                 ff_conv_b[i].reshape(1, -1), bf(ff_w_down[i]), norm_final.reshape(1, d),
                 final_norm=(i == depth - 1), tm=1024)
    return x
```

```python
import functools

import jax
import jax.numpy as jnp
from jax import lax
from jax.experimental import pallas as pl
from jax.experimental.pallas import tpu as pltpu

F32 = jnp.float32
BF16 = jnp.bfloat16

RWKV_HEAD_DIM = 64
RWKV_GN_EPS = 64e-5
SGU_CHUNK = 128
SGU_GROUPS = 16
SGU_LN_EPS = 1e-5
XATTN_HEADS = 4
RMS_EPS = 1e-6

MXU_TILE = 256
SUBLANES = 8
WKV_CHUNK = 64
WKV_PAIR = 2 * RWKV_HEAD_DIM
VMEM_LIMIT = 56 * 1024 * 1024


def _cparams(sem):
    return pltpu.CompilerParams(dimension_semantics=sem, vmem_limit_bytes=VMEM_LIMIT)


def _dot(a, b):
    return jnp.dot(a.astype(BF16), b.astype(BF16), preferred_element_type=F32)


def _dot_nt(a, b):
    return lax.dot_general(a.astype(BF16), b.astype(BF16), (((1,), (1,)), ((), ())),
                           preferred_element_type=F32)


def _dot_tn(a, b):
    return lax.dot_general(a.astype(BF16), b.astype(BF16), (((0,), (0,)), ((), ())),
                           preferred_element_type=F32)


def _split3(x):
    hi = x.astype(BF16)
    r1 = x - hi.astype(F32)
    mid = r1.astype(BF16)
    lo = (r1 - mid.astype(F32)).astype(BF16)
    return hi, mid, lo


def _dot_exact_lhs(a_bf16, x):
    return sum(jnp.dot(a_bf16, t, preferred_element_type=F32) for t in _split3(x))


def _dot_exact_rhs(x, b_bf16):
    return sum(jnp.dot(t, b_bf16, preferred_element_type=F32) for t in _split3(x))


def _rms(x, g):
    return x * lax.rsqrt(jnp.mean(x * x, axis=-1, keepdims=True) + RMS_EPS) * g


def _group_sum(x, ones_bd):
    d = x.shape[-1]
    parts = [_dot(x[:, j:j + MXU_TILE], ones_bd) for j in range(0, d, MXU_TILE)]
    return jnp.concatenate(parts, axis=-1)


def _iota(shape, dim):
    return lax.broadcasted_iota(jnp.int32, shape, dim)


def _mem_kv_kernel(mem_ref, g_ref, w_ref, o_ref):
    mn = _rms(mem_ref[...], g_ref[...])
    o_ref[...] = _dot(mn, w_ref[...]).astype(o_ref.dtype)


def _mem_kv(mem, mem_norm, w_kv):
    b, m, d = mem.shape
    nl = w_kv.shape[0]
    return pl.pallas_call(
        _mem_kv_kernel,
        grid=(nl, b),
        in_specs=[pl.BlockSpec((None, m, d), lambda l, i: (i, 0, 0)),
                  pl.BlockSpec((1, d), lambda l, i: (0, 0)),
                  pl.BlockSpec((None, d, 2 * d), lambda l, i: (l, 0, 0))],
        out_specs=pl.BlockSpec((None, None, m, 2 * d), lambda l, i: (l, i, 0, 0)),
        out_shape=jax.ShapeDtypeStruct((nl, b, m, 2 * d), BF16),
        compiler_params=_cparams(("arbitrary", "arbitrary")),
        name="mem_kv",
    )(mem, mem_norm.reshape(1, d), w_kv.astype(BF16))


def _rwkv_proj_kernel(x_ref, xp_ref, vec_ref, ones_ref, wrkv_ref, w1_ref, w2_ref, a1_ref, a2_ref,
                      g1_ref, g2_ref,
                      r_ref, lw_ref, k_ref, v_ref, kk_ref, b_ref, g_ref, bonus_ref):
    i = pl.program_id(1)
    vec = vec_ref[...]
    row = lambda n: vec[n:n + 1, :]
    gn, w0, a0, k_k, k_a, r_k = row(0), row(7), row(8), row(9), row(10), row(11)
    h = _rms(x_ref[...], gn)
    hp = _rms(xp_ref[...], gn)[SUBLANES - 1:SUBLANES, :]
    hp = jnp.where(i == 0, 0.0, hp)
    rows = _iota(h.shape, 0)
    h_prev = jnp.where(rows == 0, hp, pltpu.roll(h, 1, axis=0))
    dx = h_prev - h
    mix = lambda j: (h + dx * row(1 + j)).astype(BF16)
    r = _dot(mix(0), wrkv_ref[0])
    k = _dot(mix(2), wrkv_ref[1])
    v = _dot(mix(3), wrkv_ref[2])
    wl = _dot(jnp.tanh(_dot(mix(1), w1_ref[...])), w2_ref[...])
    w_log = -jax.nn.softplus(-(w0 + wl)) - 0.5
    lw = -jnp.exp(w_log)
    a = jax.nn.sigmoid(a0 + _dot(_dot(mix(4), a1_ref[...]), a2_ref[...]))
    g = _dot(jax.nn.sigmoid(_dot(mix(5), g1_ref[...])), g2_ref[...])
    ones_bd = ones_ref[...]
    kk_raw = k * k_k
    kk_norm = jnp.sqrt(_group_sum(kk_raw * kk_raw, ones_bd))
    kk = kk_raw / jnp.maximum(kk_norm, 1e-12)
    k2 = k * (1.0 + (a - 1.0) * k_a)
    bonus = _group_sum(r * k2 * r_k, ones_bd) * v
    r_ref[...] = r.astype(r_ref.dtype)
    lw_ref[...] = lw
    k_ref[...] = k2.astype(k_ref.dtype)
    v_ref[...] = v.astype(v_ref.dtype)
    kk_ref[...] = kk.astype(kk_ref.dtype)
    b_ref[...] = (kk * a).astype(b_ref.dtype)
    g_ref[...] = g.astype(g_ref.dtype)
    bonus_ref[...] = bonus.astype(bonus_ref.dtype)


def _rwkv_proj(x, vecs, ones_bd, w_rkv, w1, w2, a1, a2, g1, g2, tm):
    b, s, d = x.shape
    tm = min(tm, s)
    tok = pl.BlockSpec((None, tm, d), lambda bi, i: (bi, i, 0))
    full = lambda arr: pl.BlockSpec(arr.shape, lambda bi, i: (0,) * arr.ndim)
    prev = pl.BlockSpec((None, SUBLANES, d),
                        lambda bi, i: (bi, jnp.maximum(i * (tm // SUBLANES) - 1, 0), 0))
    weights = (vecs, ones_bd, w_rkv, w1, w2, a1, a2, g1, g2)
    out_dt = (BF16, F32, BF16, BF16, BF16, BF16, BF16, BF16)
    return pl.pallas_call(
        _rwkv_proj_kernel,
        grid=(b, s // tm),
        in_specs=[tok, prev] + [full(w) for w in weights],
        out_specs=[tok] * 8,
        out_shape=[jax.ShapeDtypeStruct((b, s, d), dt) for dt in out_dt],
        compiler_params=_cparams(("arbitrary", "arbitrary")),
        name="rwkv_proj",
    )(x, x, *weights)


def _wkv_kernel(r_ref, lw_ref, k_ref, v_ref, kk_ref, b_ref, tri_ref, o_ref, st_ref,
                *, n_chunks, n_pairs):
    L, W = WKV_CHUNK, WKV_PAIR
    nh = W // RWKV_HEAD_DIM

    @pl.when(pl.program_id(2) == 0)
    def _():
        st_ref[...] = jnp.zeros_like(st_ref)

    ri = _iota((W, W), 0)
    ci = _iota((W, W), 1)
    strict = ri > ci
    diag = ri == ci
    incl2 = _iota((W, 2 * W), 0) >= (_iota((W, 2 * W), 1) % W)
    lane_head = _iota((L, W), 1) // RWKV_HEAD_DIM
    tri = tri_ref[...]

    def stack(x):
        return jnp.concatenate([jnp.where(lane_head == h, x, 0.0) for h in range(nh)],
                               axis=0).astype(BF16)

    units = [(c, p) for c in range(n_chunks) for p in range(n_pairs)]
    nu = len(units)
    at, rt, vs, bk, bkh, wlast = [], [], [], [], [], []
    for c in range(n_chunks):
        rows = slice(c * L, (c + 1) * L)
        lw = lw_ref[rows, :]
        cum = _dot_exact_lhs(tri, lw)
        cum_last = cum[L - 1:L, :]
        w_inv = jnp.exp(-cum)
        w_tail = jnp.exp(cum_last - cum)
        w_last = jnp.exp(cum_last)
        k = k_ref[rows, :].astype(F32)
        b = b_ref[rows, :].astype(F32)
        rt_c = r_ref[rows, :].astype(F32) * jnp.exp(cum)
        at_c = -kk_ref[rows, :].astype(F32) * jnp.exp(cum - lw)
        v_c = v_ref[rows, :].astype(F32)
        bt_c, kt_c, bh_c, kh_c = b * w_inv, k * w_inv, b * w_tail, k * w_tail
        for p in range(n_pairs):
            ln = slice(p * W, (p + 1) * W)
            rt.append(stack(rt_c[:, ln]))
            at.append(stack(at_c[:, ln]))
            vs.append(stack(v_c[:, ln]))
            bk.append(jnp.concatenate([stack(bt_c[:, ln]), stack(kt_c[:, ln])], axis=0))
            bkh.append(jnp.concatenate([stack(bh_c[:, ln]), stack(kh_c[:, ln])], axis=0))
            wlast.append(w_last[:, ln])
    a_all = [_dot_nt(at[u], bk[u]) for u in range(nu)]
    r_all = [_dot_nt(rt[u], bk[u]) for u in range(nu)]
    nil = [jnp.where(strict, a_all[u][:, :W], 0.0) for u in range(nu)]
    a_ak = [jnp.where(strict, a_all[u][:, W:], 0.0).astype(BF16) for u in range(nu)]
    a_r = [jnp.where(incl2, r_all[u], 0.0).astype(BF16) for u in range(nu)]
    av = [_dot(a_ak[u], vs[u]).astype(BF16) for u in range(nu)]
    t = [jnp.where(diag, 1.0, nil[u]) for u in range(nu)]
    pw = [nil[u].astype(BF16) for u in range(nu)]
    pw = [_dot(pw[u], pw[u]).astype(BF16) for u in range(nu)]
    for _ in range(4):
        both = [_dot(pw[u], jnp.concatenate([pw[u], t[u].astype(BF16)], axis=1))
                for u in range(nu)]
        pw = [both[u][:, :W].astype(BF16) for u in range(nu)]
        t = [t[u] + both[u][:, W:] for u in range(nu)]
    t = [t[u] + _dot(pw[u], t[u]) for u in range(nu)]
    pu = [_dot(t[u], jnp.concatenate([at[u], av[u]], axis=1)).astype(BF16)
          for u in range(nu)]
    zero = jnp.zeros((W, W), BF16)
    rhs = [jnp.concatenate([pu[u], jnp.concatenate([zero, vs[u]], axis=1)], axis=0)
           for u in range(nu)]
    qo = [_dot(a_r[u], rhs[u]) for u in range(nu)]
    gh = [_dot_tn(bkh[u], rhs[u]) for u in range(nu)]
    qg_lhs = [jnp.concatenate(
        [(rt[u].astype(F32) + qo[u][:, :W]).astype(BF16),
         (jnp.where(diag, wlast[u], 0.0) + gh[u][:, :W]).astype(BF16)], axis=0)
        for u in range(nu)]
    states = [st_ref[p] for p in range(n_pairs)]
    for u, (c, p) in enumerate(units):
        qg = _dot(qg_lhs[u], states[p])
        o_s = qg[:W] + qo[u][:, W:]
        o_ref[c * L:(c + 1) * L, p * W:(p + 1) * W] = (o_s[:L] + o_s[L:]).astype(o_ref.dtype)
        states[p] = qg[W:] + gh[u][:, W:]
    for p in range(n_pairs):
        st_ref[p] = states[p]


def _wkv(r, lw, k, v, kk, b, tc, lanes):
    bsz, s, d = r.shape
    tc = min(tc, s)
    n_chunks = tc // WKV_CHUNK
    W = min(lanes, d)
    n_pairs = W // WKV_PAIR
    tri = (jnp.arange(WKV_CHUNK)[:, None] >= jnp.arange(WKV_CHUNK)[None, :]).astype(BF16)
    blk = pl.BlockSpec((None, tc, W), lambda bi, q, i: (bi, i, q))
    return pl.pallas_call(
        functools.partial(_wkv_kernel, n_chunks=n_chunks, n_pairs=n_pairs),
        grid=(bsz, d // W, s // tc),
        in_specs=[blk] * 6 + [pl.BlockSpec(tri.shape, lambda bi, q, i: (0, 0))],
        out_specs=blk,
        out_shape=jax.ShapeDtypeStruct((bsz, s, d), BF16),
        scratch_shapes=[pltpu.VMEM((n_pairs, WKV_PAIR, WKV_PAIR), F32)],
        compiler_params=_cparams(("arbitrary", "arbitrary", "arbitrary")),
        name="wkv",
    )(r, lw, k, v, kk, b, tri)


def _rwkv_out_kernel(x_ref, o_ref, g_ref, bonus_ref, vec_ref, ones_ref, wo_ref, out_ref):
    vec = vec_ref[...]
    lnx_g, lnx_b = vec[0:1, :], vec[1:2, :]
    ones_bd = ones_ref[...]
    o = o_ref[...].astype(F32)
    inv_n = 1.0 / RWKV_HEAD_DIM
    mean = _group_sum(o, ones_bd) * inv_n
    dlt = o - mean
    var = _group_sum(dlt * dlt, ones_bd) * inv_n
    y = dlt * lax.rsqrt(var + RWKV_GN_EPS) * lnx_g + lnx_b + bonus_ref[...].astype(F32)
    y = y * g_ref[...].astype(F32)
    out_ref[...] = x_ref[...] + _dot(y, wo_ref[...])


def _rwkv_out(x, o, g, bonus, vecs, ones_bd, w_o, tm):
    b, s, d = x.shape
    tm = min(tm, s)
    tok = pl.BlockSpec((None, tm, d), lambda bi, i: (bi, i, 0))
    full = lambda arr: pl.BlockSpec(arr.shape, lambda bi, i: (0,) * arr.ndim)
    return pl.pallas_call(
        _rwkv_out_kernel,
        grid=(b, s // tm),
        in_specs=[tok, tok, tok, tok, full(vecs), full(ones_bd), full(w_o)],
        out_specs=tok,
        out_shape=jax.ShapeDtypeStruct((b, s, d), F32),
        compiler_params=_cparams(("arbitrary", "arbitrary")),
        name="rwkv_out",
    )(x, o, g, bonus, vecs, ones_bd, w_o)


def _xattn_kernel(x_ref, g_ref, wq_ref, kv_ref, wo_ref, out_ref):
    x = x_ref[...]
    d = x.shape[-1]
    hd = d // XATTN_HEADS
    q = _dot(_rms(x, g_ref[...]), wq_ref[...])
    outs = []
    for h in range(XATTN_HEADS):
        kh = kv_ref[:, h * hd:(h + 1) * hd]
        vh = kv_ref[:, d + h * hd:d + (h + 1) * hd]
        s = _dot_nt(q[:, h * hd:(h + 1) * hd], kh) * (hd ** -0.5)
        p = jnp.exp(s - jnp.max(s, axis=-1, keepdims=True))
        p = p / jnp.sum(p, axis=-1, keepdims=True)
        outs.append(_dot(p, vh).astype(BF16))
    out_ref[...] = x + _dot(jnp.concatenate(outs, axis=-1), wo_ref[...])


def _xattn(x, g, w_q, kv, w_o, tm):
    b, s, d = x.shape
    tm = min(tm, s)
    m = kv.shape[1]
    tok = pl.BlockSpec((None, tm, d), lambda bi, i: (bi, i, 0))
    full = lambda arr: pl.BlockSpec(arr.shape, lambda bi, i: (0,) * arr.ndim)
    return pl.pallas_call(
        _xattn_kernel,
        grid=(b, s // tm),
        in_specs=[tok, full(g), full(w_q),
                  pl.BlockSpec((None, m, 2 * d), lambda bi, i: (bi, 0, 0)), full(w_o)],
        out_specs=tok,
        out_shape=jax.ShapeDtypeStruct((b, s, d), F32),
        compiler_params=_cparams(("arbitrary", "arbitrary")),
        name="xattn",
    )(x, g, w_q, kv, w_o)


def _ffn_kernel(x_ref, g_ref, wg_ref, wv_ref, cw_ref, cb_ref, wd_ref, gf_ref, out_ref,
                xn_sc, carry_sc, *, tiles_per_seq, final_norm):
    t = pl.program_id(0)
    j = pl.program_id(1)
    tm = x_ref.shape[0]

    @pl.when(j == 0)
    def _():
        x = x_ref[...]
        xn_sc[...] = _rms(x, g_ref[...]).astype(BF16)
        out_ref[...] = x

    xn = xn_sc[...]
    gate = jnp.dot(xn, wg_ref[...], preferred_element_type=F32)
    val = jnp.dot(xn, wv_ref[...], preferred_element_type=F32)
    @pl.when((t % tiles_per_seq) == 0)
    def _():
        carry_sc[j] = jnp.zeros(carry_sc.shape[1:], F32)

    prev = carry_sc[j]
    carry_sc[j] = gate[tm - SUBLANES:tm, :]
    rows = _iota(gate.shape, 0)
    s1 = jnp.where(rows == 0, prev[SUBLANES - 1:SUBLANES, :], pltpu.roll(gate, 1, axis=0))
    s2 = pltpu.roll(gate, 2, axis=0)
    s2 = jnp.where(rows == 0, prev[SUBLANES - 2:SUBLANES - 1, :], s2)
    s2 = jnp.where(rows == 1, prev[SUBLANES - 1:SUBLANES, :], s2)
    cw = cw_ref[...]
    conv = s2 * cw[0:1, :] + s1 * cw[1:2, :] + gate * cw[2:3, :] + cb_ref[...]
    act = jax.nn.silu(conv) * val
    out_ref[...] += _dot(act, wd_ref[...])

    if final_norm:
        @pl.when(j == pl.num_programs(1) - 1)
        def _():
            out_ref[...] = _rms(out_ref[...], gf_ref[...])


def _ffn(x, g, w_up, conv_w, conv_b, w_down, g_final, final_norm, tm):
    b, s, d = x.shape
    f = w_down.shape[0]
    tm = min(tm, s)
    fc = MXU_TILE
    nj = f // fc
    xf = x.reshape(b * s, d)
    tok = pl.BlockSpec((tm, d), lambda t, j: (t, 0))
    vec = pl.BlockSpec((1, d), lambda t, j: (0, 0))
    out = pl.pallas_call(
        functools.partial(_ffn_kernel, tiles_per_seq=s // tm, final_norm=final_norm),
        grid=(b * s // tm, nj),
        in_specs=[tok, vec,
                  pl.BlockSpec((d, fc), lambda t, j: (0, j)),
                  pl.BlockSpec((d, fc), lambda t, j: (0, nj + j)),
                  pl.BlockSpec((conv_w.shape[0], fc), lambda t, j: (0, j)),
                  pl.BlockSpec((1, fc), lambda t, j: (0, j)),
                  pl.BlockSpec((fc, d), lambda t, j: (j, 0)),
                  vec],
        out_specs=tok,
        out_shape=jax.ShapeDtypeStruct((b * s, d), F32),
        scratch_shapes=[pltpu.VMEM((tm, d), BF16), pltpu.VMEM((nj, SUBLANES, fc), F32)],
        compiler_params=_cparams(("arbitrary", "arbitrary")),
        name="ffn",
    )(xf, g, w_up, w_up, conv_w, conv_b, w_down, g_final)
    return out.reshape(b, s, d)


def _sgu_kernel(x_ref, vec_ref, win_ref, ws_ref, bst_ref, expand_ref, wout_ref, out_ref,
                wc_sc, bias_sc):
    d = x_ref.shape[-1]
    tm = x_ref.shape[0]
    C, G = SGU_CHUNK, SGU_GROUPS
    gd = d // G
    gpb = MXU_TILE // gd

    @pl.when((pl.program_id(0) == 0) & (pl.program_id(1) == 0))
    def _():
        causal = _iota((C, C), 0) >= _iota((C, C), 1)
        for g in range(G):
            wc_sc[g] = jnp.where(causal, ws_ref[g], 0.0).astype(BF16)
        bias_sc[...] = _dot_exact_rhs(bst_ref[...], expand_ref[...])

    vec = vec_ref[...]
    gn, ln_g, ln_b = vec[0:1, :], vec[1:2, :], vec[2:3, :]
    x = x_ref[...]
    z = jax.nn.gelu(_dot(_rms(x, gn), win_ref[...]))
    u = z[:, :d]
    v = z[:, d:]
    mean = jnp.mean(v, axis=-1, keepdims=True)
    dv = v - mean
    var = jnp.mean(dv * dv, axis=-1, keepdims=True)
    vn = (dv * lax.rsqrt(var + SGU_LN_EPS) * ln_g + ln_b).astype(BF16)
    lane_grp = _iota((C, MXU_TILE), 1) // gd
    bias = bias_sc[...]
    chunks = []
    for c in range(tm // C):
        blocks = []
        for blk in range(d // MXU_TILE):
            vb = vn[c * C:(c + 1) * C, blk * MXU_TILE:(blk + 1) * MXU_TILE]
            acc = None
            for gi in range(gpb):
                m = jnp.dot(wc_sc[blk * gpb + gi], vb, preferred_element_type=F32)
                m = jnp.where(lane_grp == gi, m, 0.0)
                acc = m if acc is None else acc + m
            blocks.append(acc)
        chunks.append(jnp.concatenate(blocks, axis=-1) + bias)
    mixed = jnp.concatenate(chunks, axis=0)
    out_ref[...] = x + _dot(u * mixed, wout_ref[...])


def _sgu(x, vecs, w_in, w_s, b_s, w_out, tm):
    b, s, d = x.shape
    tm = min(tm, s)
    gd = d // SGU_GROUPS
    expand = (jnp.arange(d)[None, :] // gd == jnp.arange(SGU_CHUNK)[:, None]).astype(BF16)
    bst = jnp.pad(b_s.T, ((0, 0), (0, SGU_CHUNK - SGU_GROUPS)))
    tok = pl.BlockSpec((None, tm, d), lambda bi, i: (bi, i, 0))
    full = lambda arr: pl.BlockSpec(arr.shape, lambda bi, i: (0,) * arr.ndim)
    ins = (vecs, w_in, w_s, bst, expand, w_out)
    return pl.pallas_call(
        _sgu_kernel,
        grid=(b, s // tm),
        in_specs=[tok] + [full(a) for a in ins],
        out_specs=tok,
        out_shape=jax.ShapeDtypeStruct((b, s, d), F32),
        scratch_shapes=[pltpu.VMEM((SGU_GROUPS, SGU_CHUNK, SGU_CHUNK), BF16),
                        pltpu.VMEM((SGU_CHUNK, d), F32)],
        compiler_params=_cparams(("arbitrary", "arbitrary")),
        name="sgu",
    )(x, *ins)


def _pad_rows(rows, d):
    n = -(-len(rows) // SUBLANES) * SUBLANES
    arr = jnp.stack([r.reshape(d).astype(F32) for r in rows])
    return jnp.pad(arr, ((0, n - len(rows)), (0, 0)))


def kernel(x, mem, norm_mix, norm_mem, norm_ffn, norm_final, mem_norm, rw_mu, rw_w_rkv, rw_w0, rw_w1, rw_w2, rw_a0, rw_a1, rw_a2, rw_g1, rw_g2, rw_k_k, rw_k_a, rw_r_k, rw_lnx_g, rw_lnx_b, rw_w_o, gm_w_in, gm_ln_g, gm_ln_b, gm_w_s, gm_b_s, gm_w_out, ca_w_q, ca_w_kv, ca_w_o, ff_w_up, ff_conv_w, ff_conv_b, ff_w_down):
    d = x.shape[-1]
    depth = norm_mix.shape[0]
    bf = lambda w: w.astype(BF16)
    lane = jnp.arange(MXU_TILE)
    ones_bd = (lane[:, None] // RWKV_HEAD_DIM == lane[None, :] // RWKV_HEAD_DIM).astype(BF16)
    kv = _mem_kv(mem, mem_norm, ca_w_kv)
    for i in range(depth):
        j = i // 2
        if i % 2 == 0:
            vecs = _pad_rows([norm_mix[i]] + [rw_mu[j, n] for n in range(6)]
                             + [rw_w0[j], rw_a0[j], rw_k_k[j], rw_k_a[j], rw_r_k[j]], d)
            r, lw, k, v, kk, b, g, bonus = _rwkv_proj(
                x, vecs, ones_bd, bf(rw_w_rkv[j]), bf(rw_w1[j]), bf(rw_w2[j]), bf(rw_a1[j]),
                bf(rw_a2[j]), bf(rw_g1[j]), bf(rw_g2[j]), tm=256)
            o = _wkv(r, lw, k, v, kk, b, tc=128, lanes=1024)
            x = _rwkv_out(x, o, g, bonus, _pad_rows([rw_lnx_g[j], rw_lnx_b[j]], d), ones_bd,
                          bf(rw_w_o[j]), tm=512)
        else:
            x = _sgu(x, _pad_rows([norm_mix[i], gm_ln_g[j], gm_ln_b[j]], d), bf(gm_w_in[j]),
                     gm_w_s[j], gm_b_s[j], bf(gm_w_out[j]), tm=512)
        x = _xattn(x, norm_mem[i].reshape(1, d), bf(ca_w_q[i]), kv[i], bf(ca_w_o[i]), tm=512)
        x = _ffn(x, norm_ffn[i].reshape(1, d), bf(ff_w_up[i]), ff_conv_w[i],
                 ff_conv_b[i].reshape(1, -1), bf(ff_w_down[i]), norm_final.reshape(1, d),
                 final_norm=(i == depth - 1), tm=1024)
    return x
```

```python
import functools
import math

import jax
import jax.numpy as jnp
from jax import lax
from jax.experimental import pallas as pl
from jax.experimental.pallas import tpu as pltpu

F32 = jnp.float32
BF16 = jnp.bfloat16

RWKV_HEAD_DIM = 64
RWKV_GN_EPS = 64e-5
SGU_CHUNK = 128
SGU_GROUPS = 16
SGU_LN_EPS = 1e-5
XATTN_HEADS = 4
RMS_EPS = 1e-6

MXU_TILE = 256
SUBLANES = 8
WKV_CHUNK = 64
WKV_PAIR = 2 * RWKV_HEAD_DIM
VMEM_LIMIT = 56 * 1024 * 1024


def _cparams(sem):
    return pltpu.CompilerParams(dimension_semantics=sem, vmem_limit_bytes=VMEM_LIMIT)


def _dot(a, b):
    return jnp.dot(a.astype(BF16), b.astype(BF16), preferred_element_type=F32)


def _dot_nt(a, b):
    return lax.dot_general(a.astype(BF16), b.astype(BF16), (((1,), (1,)), ((), ())),
                           preferred_element_type=F32)


def _dot_tn(a, b):
    return lax.dot_general(a.astype(BF16), b.astype(BF16), (((0,), (0,)), ((), ())),
                           preferred_element_type=F32)


def _split3(x):
    hi = x.astype(BF16)
    r1 = x - hi.astype(F32)
    mid = r1.astype(BF16)
    lo = (r1 - mid.astype(F32)).astype(BF16)
    return hi, mid, lo


def _dot_exact_lhs(a_bf16, x):
    return sum(jnp.dot(a_bf16, t, preferred_element_type=F32) for t in _split3(x))


def _dot_exact_rhs(x, b_bf16):
    return sum(jnp.dot(t, b_bf16, preferred_element_type=F32) for t in _split3(x))


def _rms(x, g):
    return x * lax.rsqrt(jnp.mean(x * x, axis=-1, keepdims=True) + RMS_EPS) * g


def _group_sum(x, ones_bd):
    d = x.shape[-1]
    parts = [_dot(x[:, j:j + MXU_TILE], ones_bd) for j in range(0, d, MXU_TILE)]
    return jnp.concatenate(parts, axis=-1)


def _iota(shape, dim):
    return lax.broadcasted_iota(jnp.int32, shape, dim)


def _mem_kv_kernel(mem_ref, g_ref, w_ref, o_ref):
    mn = _rms(mem_ref[...], g_ref[...])
    o_ref[...] = _dot(mn, w_ref[...]).astype(o_ref.dtype)


def _mem_kv(mem, mem_norm, w_kv):
    b, m, d = mem.shape
    nl = w_kv.shape[0]
    return pl.pallas_call(
        _mem_kv_kernel,
        grid=(nl, b),
        in_specs=[pl.BlockSpec((None, m, d), lambda l, i: (i, 0, 0)),
                  pl.BlockSpec((1, d), lambda l, i: (0, 0)),
                  pl.BlockSpec((None, d, 2 * d), lambda l, i: (l, 0, 0))],
        out_specs=pl.BlockSpec((None, None, m, 2 * d), lambda l, i: (l, i, 0, 0)),
        out_shape=jax.ShapeDtypeStruct((nl, b, m, 2 * d), BF16),
        compiler_params=_cparams(("arbitrary", "arbitrary")),
        name="mem_kv",
    )(mem, mem_norm.reshape(1, d), w_kv.astype(BF16))


def _rwkv_proj_kernel(x_ref, xp_ref, vec_ref, ones_ref, wrkv_ref, w1_ref, w2_ref, a1_ref, a2_ref,
                      g1_ref, g2_ref,
                      r_ref, lw_ref, k_ref, v_ref, kk_ref, b_ref, g_ref, bonus_ref):
    i = pl.program_id(1)
    vec = vec_ref[...]
    row = lambda n: vec[n:n + 1, :]
    gn, w0, a0, k_k, k_a, r_k = row(0), row(7), row(8), row(9), row(10), row(11)
    h = _rms(x_ref[...], gn)
    hp = _rms(xp_ref[...], gn)[SUBLANES - 1:SUBLANES, :]
    hp = jnp.where(i == 0, 0.0, hp)
    rows = _iota(h.shape, 0)
    h_prev = jnp.where(rows == 0, hp, pltpu.roll(h, 1, axis=0))
    dx = h_prev - h
    mix = lambda j: (h + dx * row(1 + j)).astype(BF16)
    r = _dot(mix(0), wrkv_ref[0])
    k = _dot(mix(2), wrkv_ref[1])
    v = _dot(mix(3), wrkv_ref[2])
    wl = _dot(jnp.tanh(_dot(mix(1), w1_ref[...])), w2_ref[...])
    lw = -math.exp(-0.5) * jax.nn.sigmoid(w0 + wl)
    a = jax.nn.sigmoid(a0 + _dot(_dot(mix(4), a1_ref[...]), a2_ref[...]))
    g = _dot(jax.nn.sigmoid(_dot(mix(5), g1_ref[...])), g2_ref[...])
    ones_bd = ones_ref[...]
    kk_raw = k * k_k
    kk_norm = jnp.sqrt(_group_sum(kk_raw * kk_raw, ones_bd))
    kk = kk_raw / jnp.maximum(kk_norm, 1e-12)
    k2 = k * (1.0 + (a - 1.0) * k_a)
    bonus = _group_sum(r * k2 * r_k, ones_bd) * v
    r_ref[...] = r.astype(r_ref.dtype)
    lw_ref[...] = lw
    k_ref[...] = k2.astype(k_ref.dtype)
    v_ref[...] = v.astype(v_ref.dtype)
    kk_ref[...] = kk.astype(kk_ref.dtype)
    b_ref[...] = (kk * a).astype(b_ref.dtype)
    g_ref[...] = g.astype(g_ref.dtype)
    bonus_ref[...] = bonus.astype(bonus_ref.dtype)


def _rwkv_proj(x, vecs, ones_bd, w_rkv, w1, w2, a1, a2, g1, g2, tm):
    b, s, d = x.shape
    tm = min(tm, s)
    tok = pl.BlockSpec((None, tm, d), lambda bi, i: (bi, i, 0))
    full = lambda arr: pl.BlockSpec(arr.shape, lambda bi, i: (0,) * arr.ndim)
    prev = pl.BlockSpec((None, SUBLANES, d),
                        lambda bi, i: (bi, jnp.maximum(i * (tm // SUBLANES) - 1, 0), 0))
    weights = (vecs, ones_bd, w_rkv, w1, w2, a1, a2, g1, g2)
    out_dt = (BF16, F32, BF16, BF16, BF16, BF16, BF16, BF16)
    return pl.pallas_call(
        _rwkv_proj_kernel,
        grid=(b, s // tm),
        in_specs=[tok, prev] + [full(w) for w in weights],
        out_specs=[tok] * 8,
        out_shape=[jax.ShapeDtypeStruct((b, s, d), dt) for dt in out_dt],
        compiler_params=_cparams(("arbitrary", "arbitrary")),
        name="rwkv_proj",
    )(x, x, *weights)


def _wkv_kernel(r_ref, lw_ref, k_ref, v_ref, kk_ref, b_ref, tri_ref, o_ref, st_ref,
                *, n_chunks, n_pairs):
    L, W = WKV_CHUNK, WKV_PAIR
    nh = W // RWKV_HEAD_DIM

    @pl.when(pl.program_id(2) == 0)
    def _():
        st_ref[...] = jnp.zeros_like(st_ref)

    ri = _iota((W, W), 0)
    ci = _iota((W, W), 1)
    strict = ri > ci
    diag = ri == ci
    incl2 = _iota((W, 2 * W), 0) >= (_iota((W, 2 * W), 1) % W)
    lane_head = _iota((L, W), 1) // RWKV_HEAD_DIM
    tri = tri_ref[...]

    def stack(x):
        return jnp.concatenate([jnp.where(lane_head == h, x, 0.0) for h in range(nh)],
                               axis=0).astype(BF16)

    units = [(c, p) for c in range(n_chunks) for p in range(n_pairs)]
    nu = len(units)
    at, rt, vs, bk, bkh, wlast = [], [], [], [], [], []
    for c in range(n_chunks):
        rows = slice(c * L, (c + 1) * L)
        lw = lw_ref[rows, :]
        cum = _dot_exact_lhs(tri, lw)
        cum_last = cum[L - 1:L, :]
        w_inv = jnp.exp(-cum)
        w_tail = jnp.exp(cum_last - cum)
        w_last = jnp.exp(cum_last)
        k = k_ref[rows, :].astype(F32)
        b = b_ref[rows, :].astype(F32)
        rt_c = r_ref[rows, :].astype(F32) * jnp.exp(cum)
        at_c = -kk_ref[rows, :].astype(F32) * jnp.exp(cum - lw)
        v_c = v_ref[rows, :].astype(F32)
        bt_c, kt_c, bh_c, kh_c = b * w_inv, k * w_inv, b * w_tail, k * w_tail
        for p in range(n_pairs):
            ln = slice(p * W, (p + 1) * W)
            rt.append(stack(rt_c[:, ln]))
            at.append(stack(at_c[:, ln]))
            vs.append(stack(v_c[:, ln]))
            bk.append(jnp.concatenate([stack(bt_c[:, ln]), stack(kt_c[:, ln])], axis=0))
            bkh.append(jnp.concatenate([stack(bh_c[:, ln]), stack(kh_c[:, ln])], axis=0))
            wlast.append(w_last[:, ln])
    a_all = [_dot_nt(at[u], bk[u]) for u in range(nu)]
    r_all = [_dot_nt(rt[u], bk[u]) for u in range(nu)]
    nil = [jnp.where(strict, a_all[u][:, :W], 0.0) for u in range(nu)]
    a_ak = [jnp.where(strict, a_all[u][:, W:], 0.0).astype(BF16) for u in range(nu)]
    a_r = [jnp.where(incl2, r_all[u], 0.0).astype(BF16) for u in range(nu)]
    av = [_dot(a_ak[u], vs[u]).astype(BF16) for u in range(nu)]
    t = [jnp.where(diag, 1.0, nil[u]) for u in range(nu)]
    pw = [nil[u].astype(BF16) for u in range(nu)]
    pw = [_dot(pw[u], pw[u]).astype(BF16) for u in range(nu)]
    for _ in range(4):
        both = [_dot(pw[u], jnp.concatenate([pw[u], t[u].astype(BF16)], axis=1))
                for u in range(nu)]
        pw = [both[u][:, :W].astype(BF16) for u in range(nu)]
        t = [t[u] + both[u][:, W:] for u in range(nu)]
    t = [t[u] + _dot(pw[u], t[u]) for u in range(nu)]
    pu = [_dot(t[u], jnp.concatenate([at[u], av[u]], axis=1)).astype(BF16)
          for u in range(nu)]
    zero = jnp.zeros((W, W), BF16)
    rhs = [jnp.concatenate([pu[u], jnp.concatenate([zero, vs[u]], axis=1)], axis=0)
           for u in range(nu)]
    qo = [_dot(a_r[u], rhs[u]) for u in range(nu)]
    gh = [_dot_tn(bkh[u], rhs[u]) for u in range(nu)]
    qg_lhs = [jnp.concatenate(
        [(rt[u].astype(F32) + qo[u][:, :W]).astype(BF16),
         (jnp.where(diag, wlast[u], 0.0) + gh[u][:, :W]).astype(BF16)], axis=0)
        for u in range(nu)]
    states = [st_ref[p] for p in range(n_pairs)]
    for u, (c, p) in enumerate(units):
        qg = _dot(qg_lhs[u], states[p])
        o_s = qg[:W] + qo[u][:, W:]
        o_ref[c * L:(c + 1) * L, p * W:(p + 1) * W] = (o_s[:L] + o_s[L:]).astype(o_ref.dtype)
        states[p] = qg[W:] + gh[u][:, W:]
    for p in range(n_pairs):
        st_ref[p] = states[p]


def _wkv(r, lw, k, v, kk, b, tc, lanes):
    bsz, s, d = r.shape
    tc = min(tc, s)
    n_chunks = tc // WKV_CHUNK
    W = min(lanes, d)
    n_pairs = W // WKV_PAIR
    tri = (jnp.arange(WKV_CHUNK)[:, None] >= jnp.arange(WKV_CHUNK)[None, :]).astype(BF16)
    blk = pl.BlockSpec((None, tc, W), lambda bi, q, i: (bi, i, q))
    return pl.pallas_call(
        functools.partial(_wkv_kernel, n_chunks=n_chunks, n_pairs=n_pairs),
        grid=(bsz, d // W, s // tc),
        in_specs=[blk] * 6 + [pl.BlockSpec(tri.shape, lambda bi, q, i: (0, 0))],
        out_specs=blk,
        out_shape=jax.ShapeDtypeStruct((bsz, s, d), BF16),
        scratch_shapes=[pltpu.VMEM((n_pairs, WKV_PAIR, WKV_PAIR), F32)],
        compiler_params=_cparams(("arbitrary", "arbitrary", "arbitrary")),
        name="wkv",
    )(r, lw, k, v, kk, b, tri)


def _rwkv_out_kernel(x_ref, o_ref, g_ref, bonus_ref, vec_ref, ones_ref, wo_ref, out_ref):
    vec = vec_ref[...]
    lnx_g, lnx_b = vec[0:1, :], vec[1:2, :]
    ones_bd = ones_ref[...]
    o = o_ref[...].astype(F32)
    inv_n = 1.0 / RWKV_HEAD_DIM
    mean = _group_sum(o, ones_bd) * inv_n
    dlt = o - mean
    var = _group_sum(dlt * dlt, ones_bd) * inv_n
    y = dlt * lax.rsqrt(var + RWKV_GN_EPS) * lnx_g + lnx_b + bonus_ref[...].astype(F32)
    y = y * g_ref[...].astype(F32)
    out_ref[...] = x_ref[...] + _dot(y, wo_ref[...])


def _rwkv_out(x, o, g, bonus, vecs, ones_bd, w_o, tm):
    b, s, d = x.shape
    tm = min(tm, s)
    tok = pl.BlockSpec((None, tm, d), lambda bi, i: (bi, i, 0))
    full = lambda arr: pl.BlockSpec(arr.shape, lambda bi, i: (0,) * arr.ndim)
    return pl.pallas_call(
        _rwkv_out_kernel,
        grid=(b, s // tm),
        in_specs=[tok, tok, tok, tok, full(vecs), full(ones_bd), full(w_o)],
        out_specs=tok,
        out_shape=jax.ShapeDtypeStruct((b, s, d), F32),
        compiler_params=_cparams(("arbitrary", "arbitrary")),
        name="rwkv_out",
    )(x, o, g, bonus, vecs, ones_bd, w_o)


def _xattn_kernel(x_ref, g_ref, wq_ref, kv_ref, wo_ref, out_ref):
    x = x_ref[...]
    d = x.shape[-1]
    hd = d // XATTN_HEADS
    q = _dot(_rms(x, g_ref[...]), wq_ref[...])
    outs = []
    for h in range(XATTN_HEADS):
        kh = kv_ref[:, h * hd:(h + 1) * hd]
        vh = kv_ref[:, d + h * hd:d + (h + 1) * hd]
        s = _dot_nt(q[:, h * hd:(h + 1) * hd], kh) * (hd ** -0.5)
        p = jnp.exp(s - jnp.max(s, axis=-1, keepdims=True))
        p = p / jnp.sum(p, axis=-1, keepdims=True)
        outs.append(_dot(p, vh).astype(BF16))
    out_ref[...] = x + _dot(jnp.concatenate(outs, axis=-1), wo_ref[...])


def _xattn(x, g, w_q, kv, w_o, tm):
    b, s, d = x.shape
    tm = min(tm, s)
    m = kv.shape[1]
    tok = pl.BlockSpec((None, tm, d), lambda bi, i: (bi, i, 0))
    full = lambda arr: pl.BlockSpec(arr.shape, lambda bi, i: (0,) * arr.ndim)
    return pl.pallas_call(
        _xattn_kernel,
        grid=(b, s // tm),
        in_specs=[tok, full(g), full(w_q),
                  pl.BlockSpec((None, m, 2 * d), lambda bi, i: (bi, 0, 0)), full(w_o)],
        out_specs=tok,
        out_shape=jax.ShapeDtypeStruct((b, s, d), F32),
        compiler_params=_cparams(("arbitrary", "arbitrary")),
        name="xattn",
    )(x, g, w_q, kv, w_o)


def _ffn_kernel(x_ref, g_ref, wup_ref, cw_ref, cb_ref, wd_ref, gf_ref, out_ref,
                act_sc, carry_sc, *, tiles_per_seq, final_norm):
    tm = x_ref.shape[0]
    f = wd_ref.shape[0]
    fc = MXU_TILE

    @pl.when((pl.program_id(0) % tiles_per_seq) == 0)
    def _():
        carry_sc[...] = jnp.zeros_like(carry_sc)

    x = x_ref[...]
    xn = _rms(x, g_ref[...]).astype(BF16)
    rows = _iota((tm, fc), 0)
    for j in range(f // fc):
        cols = slice(j * fc, (j + 1) * fc)
        gate = jnp.dot(xn, wup_ref[:, cols], preferred_element_type=F32)
        val = jnp.dot(xn, wup_ref[:, f + j * fc:f + (j + 1) * fc], preferred_element_type=F32)
        prev = carry_sc[:, cols]
        carry_sc[:, cols] = gate[tm - SUBLANES:tm, :]
        s1 = jnp.where(rows == 0, prev[SUBLANES - 1:SUBLANES, :], pltpu.roll(gate, 1, axis=0))
        s2 = pltpu.roll(gate, 2, axis=0)
        s2 = jnp.where(rows == 0, prev[SUBLANES - 2:SUBLANES - 1, :], s2)
        s2 = jnp.where(rows == 1, prev[SUBLANES - 1:SUBLANES, :], s2)
        conv = (s2 * cw_ref[0:1, cols] + s1 * cw_ref[1:2, cols] + gate * cw_ref[2:3, cols]
                + cb_ref[:, cols])
        act_sc[:, cols] = (jax.nn.silu(conv) * val).astype(BF16)
    y = x + jnp.dot(act_sc[...], wd_ref[...], preferred_element_type=F32)
    out_ref[...] = _rms(y, gf_ref[...]) if final_norm else y


def _ffn(x, g, w_up, conv_w, conv_b, w_down, g_final, final_norm, tm):
    b, s, d = x.shape
    f = w_down.shape[0]
    tm = min(tm, s)
    xf = x.reshape(b * s, d)
    tok = pl.BlockSpec((tm, d), lambda t: (t, 0))
    const = lambda arr: pl.BlockSpec(arr.shape, lambda t: (0,) * arr.ndim,
                                     pipeline_mode=pl.Buffered(1))
    weights = (g, w_up, conv_w, conv_b, w_down, g_final)
    out = pl.pallas_call(
        functools.partial(_ffn_kernel, tiles_per_seq=s // tm, final_norm=final_norm),
        grid=(b * s // tm,),
        in_specs=[tok] + [const(w) for w in weights],
        out_specs=tok,
        out_shape=jax.ShapeDtypeStruct((b * s, d), F32),
        scratch_shapes=[pltpu.VMEM((tm, f), BF16), pltpu.VMEM((SUBLANES, f), F32)],
        compiler_params=_cparams(("arbitrary",)),
        name="ffn",
    )(xf, *weights)
    return out.reshape(b, s, d)


def _sgu_kernel(x_ref, vec_ref, win_ref, ws_ref, bst_ref, expand_ref, wout_ref, out_ref,
                wc_sc, bias_sc):
    d = x_ref.shape[-1]
    tm = x_ref.shape[0]
    C, G = SGU_CHUNK, SGU_GROUPS
    gd = d // G
    gpb = MXU_TILE // gd

    @pl.when((pl.program_id(0) == 0) & (pl.program_id(1) == 0))
    def _():
        causal = _iota((C, C), 0) >= _iota((C, C), 1)
        for g in range(G):
            wc_sc[g] = jnp.where(causal, ws_ref[g], 0.0).astype(BF16)
        bias_sc[...] = _dot_exact_rhs(bst_ref[...], expand_ref[...])

    vec = vec_ref[...]
    gn, ln_g, ln_b = vec[0:1, :], vec[1:2, :], vec[2:3, :]
    x = x_ref[...]
    z = jax.nn.gelu(_dot(_rms(x, gn), win_ref[...]))
    u = z[:, :d]
    v = z[:, d:]
    mean = jnp.mean(v, axis=-1, keepdims=True)
    dv = v - mean
    var = jnp.mean(dv * dv, axis=-1, keepdims=True)
    vn = (dv * lax.rsqrt(var + SGU_LN_EPS) * ln_g + ln_b).astype(BF16)
    lane_grp = _iota((C, MXU_TILE), 1) // gd
    bias = bias_sc[...]
    chunks = []
    for c in range(tm // C):
        blocks = []
        for blk in range(d // MXU_TILE):
            vb = vn[c * C:(c + 1) * C, blk * MXU_TILE:(blk + 1) * MXU_TILE]
            acc = None
            for gi in range(gpb):
                m = jnp.dot(wc_sc[blk * gpb + gi], vb, preferred_element_type=F32)
                m = jnp.where(lane_grp == gi, m, 0.0)
                acc = m if acc is None else acc + m
            blocks.append(acc)
        chunks.append(jnp.concatenate(blocks, axis=-1) + bias)
    mixed = jnp.concatenate(chunks, axis=0)
    out_ref[...] = x + _dot(u * mixed, wout_ref[...])


def _sgu(x, vecs, w_in, w_s, b_s, w_out, tm):
    b, s, d = x.shape
    tm = min(tm, s)
    gd = d // SGU_GROUPS
    expand = (jnp.arange(d)[None, :] // gd == jnp.arange(SGU_CHUNK)[:, None]).astype(BF16)
    bst = jnp.pad(b_s.T, ((0, 0), (0, SGU_CHUNK - SGU_GROUPS)))
    tok = pl.BlockSpec((None, tm, d), lambda bi, i: (bi, i, 0))
    full = lambda arr: pl.BlockSpec(arr.shape, lambda bi, i: (0,) * arr.ndim)
    ins = (vecs, w_in, w_s, bst, expand, w_out)
    return pl.pallas_call(
        _sgu_kernel,
        grid=(b, s // tm),
        in_specs=[tok] + [full(a) for a in ins],
        out_specs=tok,
        out_shape=jax.ShapeDtypeStruct((b, s, d), F32),
        scratch_shapes=[pltpu.VMEM((SGU_GROUPS, SGU_CHUNK, SGU_CHUNK), BF16),
                        pltpu.VMEM((SGU_CHUNK, d), F32)],
        compiler_params=_cparams(("arbitrary", "arbitrary")),
        name="sgu",
    )(x, *ins)


def _pad_rows(rows, d):
    n = -(-len(rows) // SUBLANES) * SUBLANES
    arr = jnp.stack([r.reshape(d).astype(F32) for r in rows])
    return jnp.pad(arr, ((0, n - len(rows)), (0, 0)))


def kernel(x, mem, norm_mix, norm_mem, norm_ffn, norm_final, mem_norm, rw_mu, rw_w_rkv, rw_w0, rw_w1, rw_w2, rw_a0, rw_a1, rw_a2, rw_g1, rw_g2, rw_k_k, rw_k_a, rw_r_k, rw_lnx_g, rw_lnx_b, rw_w_o, gm_w_in, gm_ln_g, gm_ln_b, gm_w_s, gm_b_s, gm_w_out, ca_w_q, ca_w_kv, ca_w_o, ff_w_up, ff_conv_w, ff_conv_b, ff_w_down):
    d = x.shape[-1]
    depth = norm_mix.shape[0]
    bf = lambda w: w.astype(BF16)
    lane = jnp.arange(MXU_TILE)
    ones_bd = (lane[:, None] // RWKV_HEAD_DIM == lane[None, :] // RWKV_HEAD_DIM).astype(BF16)
    kv = _mem_kv(mem, mem_norm, ca_w_kv)
    for i in range(depth):
        j = i // 2
        if i % 2 == 0:
            vecs = _pad_rows([norm_mix[i]] + [rw_mu[j, n] for n in range(6)]
                             + [rw_w0[j], rw_a0[j], rw_k_k[j], rw_k_a[j], rw_r_k[j]], d)
            r, lw, k, v, kk, b, g, bonus = _rwkv_proj(
                x, vecs, ones_bd, bf(rw_w_rkv[j]), bf(rw_w1[j]), bf(rw_w2[j]), bf(rw_a1[j]),
                bf(rw_a2[j]), bf(rw_g1[j]), bf(rw_g2[j]), tm=256)
            o = _wkv(r, lw, k, v, kk, b, tc=128, lanes=1024)
            x = _rwkv_out(x, o, g, bonus, _pad_rows([rw_lnx_g[j], rw_lnx_b[j]], d), ones_bd,
                          bf(rw_w_o[j]), tm=512)
        else:
            x = _sgu(x, _pad_rows([norm_mix[i], gm_ln_g[j], gm_ln_b[j]], d), bf(gm_w_in[j]),
                     gm_w_s[j], gm_b_s[j], bf(gm_w_out[j]), tm=512)
        x = _xattn(x, norm_mem[i].reshape(1, d), bf(ca_w_q[i]), kv[i], bf(ca_w_o[i]), tm=512)
        x = _ffn(x, norm_ffn[i].reshape(1, d), bf(ff_w_up[i]), ff_conv_w[i],
                 ff_conv_b[i].reshape(1, -1), bf(ff_w_down[i]), norm_final.reshape(1, d),
                 final_norm=(i == depth - 1), tm=512)
    return x
```

```python
import functools
import math

import jax
import jax.numpy as jnp
from jax import lax
from jax.experimental import pallas as pl
from jax.experimental.pallas import tpu as pltpu

F32 = jnp.float32
BF16 = jnp.bfloat16

RWKV_HEAD_DIM = 64
RWKV_GN_EPS = 64e-5
SGU_CHUNK = 128
SGU_GROUPS = 16
SGU_LN_EPS = 1e-5
XATTN_HEADS = 4
RMS_EPS = 1e-6

MXU_TILE = 256
SUBLANES = 8
WKV_CHUNK = 64
WKV_PAIR = 2 * RWKV_HEAD_DIM
VMEM_LIMIT = 56 * 1024 * 1024


def _cparams(sem):
    return pltpu.CompilerParams(dimension_semantics=sem, vmem_limit_bytes=VMEM_LIMIT)


def _dot(a, b):
    return jnp.dot(a.astype(BF16), b.astype(BF16), preferred_element_type=F32)


def _dot_nt(a, b):
    return lax.dot_general(a.astype(BF16), b.astype(BF16), (((1,), (1,)), ((), ())),
                           preferred_element_type=F32)


def _dot_tn(a, b):
    return lax.dot_general(a.astype(BF16), b.astype(BF16), (((0,), (0,)), ((), ())),
                           preferred_element_type=F32)


def _split3(x):
    hi = x.astype(BF16)
    r1 = x - hi.astype(F32)
    mid = r1.astype(BF16)
    lo = (r1 - mid.astype(F32)).astype(BF16)
    return hi, mid, lo


def _dot_exact_lhs(a_bf16, x):
    return sum(jnp.dot(a_bf16, t, preferred_element_type=F32) for t in _split3(x))


def _dot_exact_rhs(x, b_bf16):
    return sum(jnp.dot(t, b_bf16, preferred_element_type=F32) for t in _split3(x))


def _rms(x, g):
    return x * lax.rsqrt(jnp.mean(x * x, axis=-1, keepdims=True) + RMS_EPS) * g


def _sigmoid(x):
    return 1.0 / (1.0 + jnp.exp(-x))


def _group_sum(x, ones_bd):
    d = x.shape[-1]
    parts = [_dot(x[:, j:j + MXU_TILE], ones_bd) for j in range(0, d, MXU_TILE)]
    return jnp.concatenate(parts, axis=-1)


def _iota(shape, dim):
    return lax.broadcasted_iota(jnp.int32, shape, dim)


def _mem_kv_kernel(mem_ref, g_ref, w_ref, o_ref):
    mn = _rms(mem_ref[...], g_ref[...])
    o_ref[...] = _dot(mn, w_ref[...]).astype(o_ref.dtype)


def _mem_kv(mem, mem_norm, w_kv):
    b, m, d = mem.shape
    nl = w_kv.shape[0]
    return pl.pallas_call(
        _mem_kv_kernel,
        grid=(nl, b),
        in_specs=[pl.BlockSpec((None, m, d), lambda l, i: (i, 0, 0)),
                  pl.BlockSpec((1, d), lambda l, i: (0, 0)),
                  pl.BlockSpec((None, d, 2 * d), lambda l, i: (l, 0, 0))],
        out_specs=pl.BlockSpec((None, None, m, 2 * d), lambda l, i: (l, i, 0, 0)),
        out_shape=jax.ShapeDtypeStruct((nl, b, m, 2 * d), BF16),
        compiler_params=_cparams(("arbitrary", "arbitrary")),
        name="mem_kv",
    )(mem, mem_norm.reshape(1, d), w_kv.astype(BF16))


def _rwkv_proj_kernel(x_ref, xp_ref, vec_ref, ones_ref, wrkv_ref, w1_ref, w2_ref, a1_ref, a2_ref,
                      g1_ref, g2_ref,
                      r_ref, lw_ref, k_ref, v_ref, kk_ref, b_ref, g_ref, bonus_ref, *, n_sub):
    i = pl.program_id(1)
    vec = vec_ref[...]
    row = lambda n: vec[n:n + 1, :]
    gn, w0, a0, k_k, k_a, r_k = row(0), row(7), row(8), row(9), row(10), row(11)
    ones_bd = ones_ref[...]
    ts = x_ref.shape[0] // n_sub
    subs = range(n_sub)
    rsl = [slice(u * ts, (u + 1) * ts) for u in subs]
    h = [_rms(x_ref[rsl[u], :], gn) for u in subs]
    hp = _rms(xp_ref[...], gn)[SUBLANES - 1:SUBLANES, :]
    hp = jnp.where(i == 0, 0.0, hp)
    first = _iota((ts, h[0].shape[1]), 0) == 0
    lasts = [hp] + [h[u][ts - 1:ts, :] for u in subs[:-1]]
    dx = [jnp.where(first, lasts[u], pltpu.roll(h[u], 1, axis=0)) - h[u] for u in subs]
    hb = [h[u].astype(BF16) for u in subs]
    dxb = [dx[u].astype(BF16) for u in subs]
    mix = lambda u, j: hb[u] + dxb[u] * row(1 + j).astype(BF16)
    r = [_dot(mix(u, 0), wrkv_ref[0]) for u in subs]
    k = [_dot(mix(u, 2), wrkv_ref[1]) for u in subs]
    v = [_dot(mix(u, 3), wrkv_ref[2]) for u in subs]
    wl = [_dot(jnp.tanh(_dot(mix(u, 1), w1_ref[...])), w2_ref[...]) for u in subs]
    al = [_dot(_dot(mix(u, 4), a1_ref[...]), a2_ref[...]) for u in subs]
    g = [_dot(_sigmoid(_dot(mix(u, 5), g1_ref[...])), g2_ref[...]) for u in subs]
    for u in subs:
        lw_ref[rsl[u], :] = -math.exp(-0.5) * _sigmoid(w0 + wl[u])
        a = _sigmoid(a0 + al[u])
        kk_raw = k[u] * k_k
        kk = kk_raw * lax.rsqrt(jnp.maximum(_group_sum(kk_raw * kk_raw, ones_bd), 1e-24))
        k2 = k[u] * (1.0 + (a - 1.0) * k_a)
        bonus = _group_sum(r[u] * k2 * r_k, ones_bd) * v[u]
        r_ref[rsl[u], :] = r[u].astype(r_ref.dtype)
        k_ref[rsl[u], :] = k2.astype(k_ref.dtype)
        v_ref[rsl[u], :] = v[u].astype(v_ref.dtype)
        kk_ref[rsl[u], :] = kk.astype(kk_ref.dtype)
        b_ref[rsl[u], :] = (kk * a).astype(b_ref.dtype)
        g_ref[rsl[u], :] = g[u].astype(g_ref.dtype)
        bonus_ref[rsl[u], :] = bonus.astype(bonus_ref.dtype)


def _rwkv_proj(x, vecs, ones_bd, w_rkv, w1, w2, a1, a2, g1, g2, tm, n_sub):
    b, s, d = x.shape
    tm = min(tm, s)
    tok = pl.BlockSpec((None, tm, d), lambda bi, i: (bi, i, 0))
    full = lambda arr: pl.BlockSpec(arr.shape, lambda bi, i: (0,) * arr.ndim,
                                    pipeline_mode=pl.Buffered(1))
    prev = pl.BlockSpec((None, SUBLANES, d),
                        lambda bi, i: (bi, jnp.maximum(i * (tm // SUBLANES) - 1, 0), 0))
    weights = (vecs, ones_bd, w_rkv, w1, w2, a1, a2, g1, g2)
    out_dt = (BF16, F32, BF16, BF16, BF16, BF16, BF16, BF16)
    return pl.pallas_call(
        functools.partial(_rwkv_proj_kernel, n_sub=n_sub),
        grid=(b, s // tm),
        in_specs=[tok, prev] + [full(w) for w in weights],
        out_specs=[tok] * 8,
        out_shape=[jax.ShapeDtypeStruct((b, s, d), dt) for dt in out_dt],
        compiler_params=_cparams(("arbitrary", "arbitrary")),
        name="rwkv_proj",
    )(x, x, *weights)


def _wkv_kernel(r_ref, lw_ref, k_ref, v_ref, kk_ref, b_ref, tri_ref, o_ref, st_ref,
                *, n_chunks, n_pairs):
    L, W = WKV_CHUNK, WKV_PAIR
    nh = W // RWKV_HEAD_DIM

    @pl.when(pl.program_id(2) == 0)
    def _():
        st_ref[...] = jnp.zeros_like(st_ref)

    ri = _iota((W, W), 0)
    ci = _iota((W, W), 1)
    strict = ri > ci
    diag = ri == ci
    incl2 = _iota((W, 2 * W), 0) >= (_iota((W, 2 * W), 1) % W)
    lane_head = _iota((L, W), 1) // RWKV_HEAD_DIM
    tri = tri_ref[...]

    def stack(x):
        return jnp.concatenate([jnp.where(lane_head == h, x, 0.0) for h in range(nh)],
                               axis=0).astype(BF16)

    units = [(c, p) for c in range(n_chunks) for p in range(n_pairs)]
    nu = len(units)
    at, rt, vs, bk, bkh, wlast = [], [], [], [], [], []
    for c in range(n_chunks):
        rows = slice(c * L, (c + 1) * L)
        lw = lw_ref[rows, :]
        cum = _dot_exact_lhs(tri, lw)
        cum_last = cum[L - 1:L, :]
        w_inv = jnp.exp(-cum)
        w_tail = jnp.exp(cum_last - cum)
        w_last = jnp.exp(cum_last)
        k = k_ref[rows, :].astype(F32)
        b = b_ref[rows, :].astype(F32)
        rt_c = r_ref[rows, :].astype(F32) * jnp.exp(cum)
        at_c = -kk_ref[rows, :].astype(F32) * jnp.exp(cum - lw)
        v_c = v_ref[rows, :].astype(F32)
        bt_c, kt_c, bh_c, kh_c = b * w_inv, k * w_inv, b * w_tail, k * w_tail
        for p in range(n_pairs):
            ln = slice(p * W, (p + 1) * W)
            rt.append(stack(rt_c[:, ln]))
            at.append(stack(at_c[:, ln]))
            vs.append(stack(v_c[:, ln]))
            bk.append(jnp.concatenate([stack(bt_c[:, ln]), stack(kt_c[:, ln])], axis=0))
            bkh.append(jnp.concatenate([stack(bh_c[:, ln]), stack(kh_c[:, ln])], axis=0))
            wlast.append(w_last[:, ln])
    a_all = [_dot_nt(at[u], bk[u]) for u in range(nu)]
    r_all = [_dot_nt(rt[u], bk[u]) for u in range(nu)]
    nil = [jnp.where(strict, a_all[u][:, :W], 0.0) for u in range(nu)]
    a_ak = [jnp.where(strict, a_all[u][:, W:], 0.0).astype(BF16) for u in range(nu)]
    a_r = [jnp.where(incl2, r_all[u], 0.0).astype(BF16) for u in range(nu)]
    av = [_dot(a_ak[u], vs[u]).astype(BF16) for u in range(nu)]
    t = [jnp.where(diag, 1.0, nil[u]) for u in range(nu)]
    pw = [nil[u].astype(BF16) for u in range(nu)]
    pw = [_dot(pw[u], pw[u]).astype(BF16) for u in range(nu)]
    for _ in range(4):
        both = [_dot(pw[u], jnp.concatenate([pw[u], t[u].astype(BF16)], axis=1))
                for u in range(nu)]
        pw = [both[u][:, :W].astype(BF16) for u in range(nu)]
        t = [t[u] + both[u][:, W:] for u in range(nu)]
    t = [t[u] + _dot(pw[u], t[u]) for u in range(nu)]
    pu = [_dot(t[u], jnp.concatenate([at[u], av[u]], axis=1)).astype(BF16)
          for u in range(nu)]
    zero = jnp.zeros((W, W), BF16)
    rhs = [jnp.concatenate([pu[u], jnp.concatenate([zero, vs[u]], axis=1)], axis=0)
           for u in range(nu)]
    qo = [_dot(a_r[u], rhs[u]) for u in range(nu)]
    gh = [_dot_tn(bkh[u], rhs[u]) for u in range(nu)]
    qg_lhs = [jnp.concatenate(
        [(rt[u].astype(F32) + qo[u][:, :W]).astype(BF16),
         (jnp.where(diag, wlast[u], 0.0) + gh[u][:, :W]).astype(BF16)], axis=0)
        for u in range(nu)]
    states = [st_ref[p] for p in range(n_pairs)]
    for u, (c, p) in enumerate(units):
        qg = _dot(qg_lhs[u], states[p])
        o_s = qg[:W] + qo[u][:, W:]
        o_ref[c * L:(c + 1) * L, p * W:(p + 1) * W] = (o_s[:L] + o_s[L:]).astype(o_ref.dtype)
        states[p] = qg[W:] + gh[u][:, W:]
    for p in range(n_pairs):
        st_ref[p] = states[p]


def _wkv(r, lw, k, v, kk, b, tc, lanes):
    bsz, s, d = r.shape
    tc = min(tc, s)
    n_chunks = tc // WKV_CHUNK
    W = min(lanes, d)
    n_pairs = W // WKV_PAIR
    tri = (jnp.arange(WKV_CHUNK)[:, None] >= jnp.arange(WKV_CHUNK)[None, :]).astype(BF16)
    blk = pl.BlockSpec((None, tc, W), lambda bi, q, i: (bi, i, q))
    return pl.pallas_call(
        functools.partial(_wkv_kernel, n_chunks=n_chunks, n_pairs=n_pairs),
        grid=(bsz, d // W, s // tc),
        in_specs=[blk] * 6 + [pl.BlockSpec(tri.shape, lambda bi, q, i: (0, 0))],
        out_specs=blk,
        out_shape=jax.ShapeDtypeStruct((bsz, s, d), BF16),
        scratch_shapes=[pltpu.VMEM((n_pairs, WKV_PAIR, WKV_PAIR), F32)],
        compiler_params=_cparams(("arbitrary", "arbitrary", "arbitrary")),
        name="wkv",
    )(r, lw, k, v, kk, b, tri)


def _rwkv_out_kernel(x_ref, o_ref, g_ref, bonus_ref, vec_ref, ones_ref, wo_ref, out_ref):
    vec = vec_ref[...]
    lnx_g, lnx_b = vec[0:1, :], vec[1:2, :]
    ones_bd = ones_ref[...]
    o = o_ref[...].astype(F32)
    inv_n = 1.0 / RWKV_HEAD_DIM
    mean = _group_sum(o, ones_bd) * inv_n
    dlt = o - mean
    var = _group_sum(dlt * dlt, ones_bd) * inv_n
    y = dlt * lax.rsqrt(var + RWKV_GN_EPS) * lnx_g + lnx_b + bonus_ref[...].astype(F32)
    y = y * g_ref[...].astype(F32)
    out_ref[...] = x_ref[...] + _dot(y, wo_ref[...])


def _rwkv_out(x, o, g, bonus, vecs, ones_bd, w_o, tm):
    b, s, d = x.shape
    tm = min(tm, s)
    tok = pl.BlockSpec((None, tm, d), lambda bi, i: (bi, i, 0))
    full = lambda arr: pl.BlockSpec(arr.shape, lambda bi, i: (0,) * arr.ndim)
    return pl.pallas_call(
        _rwkv_out_kernel,
        grid=(b, s // tm),
        in_specs=[tok, tok, tok, tok, full(vecs), full(ones_bd), full(w_o)],
        out_specs=tok,
        out_shape=jax.ShapeDtypeStruct((b, s, d), F32),
        compiler_params=_cparams(("arbitrary", "arbitrary")),
        name="rwkv_out",
    )(x, o, g, bonus, vecs, ones_bd, w_o)


def _xattn_kernel(x_ref, g_ref, wq_ref, kv_ref, wo_ref, out_ref):
    x = x_ref[...]
    d = x.shape[-1]
    hd = d // XATTN_HEADS
    q = _dot(_rms(x, g_ref[...]), wq_ref[...])
    outs = []
    for h in range(XATTN_HEADS):
        kh = kv_ref[:, h * hd:(h + 1) * hd]
        vh = kv_ref[:, d + h * hd:d + (h + 1) * hd]
        s = _dot_nt(q[:, h * hd:(h + 1) * hd], kh) * (hd ** -0.5)
        p = jnp.exp(s - jnp.max(s, axis=-1, keepdims=True))
        p = p * (1.0 / jnp.sum(p, axis=-1, keepdims=True))
        outs.append(_dot(p, vh).astype(BF16))
    out_ref[...] = x + _dot(jnp.concatenate(outs, axis=-1), wo_ref[...])


def _xattn(x, g, w_q, kv, w_o, tm):
    b, s, d = x.shape
    tm = min(tm, s)
    m = kv.shape[1]
    tok = pl.BlockSpec((None, tm, d), lambda bi, i: (bi, i, 0))
    full = lambda arr: pl.BlockSpec(arr.shape, lambda bi, i: (0,) * arr.ndim)
    return pl.pallas_call(
        _xattn_kernel,
        grid=(b, s // tm),
        in_specs=[tok, full(g), full(w_q),
                  pl.BlockSpec((None, m, 2 * d), lambda bi, i: (bi, 0, 0)), full(w_o)],
        out_specs=tok,
        out_shape=jax.ShapeDtypeStruct((b, s, d), F32),
        compiler_params=_cparams(("arbitrary", "arbitrary")),
        name="xattn",
    )(x, g, w_q, kv, w_o)


def _ffn_kernel(x_ref, g_ref, wup_ref, cw_ref, cb_ref, wd_ref, gf_ref, out_ref,
                act_sc, carry_sc, *, tiles_per_seq, final_norm):
    tm = x_ref.shape[0]
    f = wd_ref.shape[0]
    fc = MXU_TILE

    @pl.when((pl.program_id(0) % tiles_per_seq) == 0)
    def _():
        carry_sc[...] = jnp.zeros_like(carry_sc)

    x = x_ref[...]
    xn = _rms(x, g_ref[...]).astype(BF16)
    rows = _iota((tm, fc), 0)
    for j in range(f // fc):
        cols = slice(j * fc, (j + 1) * fc)
        gate = jnp.dot(xn, wup_ref[:, cols], preferred_element_type=F32)
        val = jnp.dot(xn, wup_ref[:, f + j * fc:f + (j + 1) * fc], preferred_element_type=F32)
        prev = carry_sc[:, cols]
        carry_sc[:, cols] = gate[tm - SUBLANES:tm, :]
        s1 = jnp.where(rows == 0, prev[SUBLANES - 1:SUBLANES, :], pltpu.roll(gate, 1, axis=0))
        s2 = pltpu.roll(gate, 2, axis=0)
        s2 = jnp.where(rows == 0, prev[SUBLANES - 2:SUBLANES - 1, :], s2)
        s2 = jnp.where(rows == 1, prev[SUBLANES - 1:SUBLANES, :], s2)
        conv = (s2 * cw_ref[0:1, cols] + s1 * cw_ref[1:2, cols] + gate * cw_ref[2:3, cols]
                + cb_ref[:, cols])
        act_sc[:, cols] = (conv * _sigmoid(conv) * val).astype(BF16)
    y = x + jnp.dot(act_sc[...], wd_ref[...], preferred_element_type=F32)
    out_ref[...] = _rms(y, gf_ref[...]) if final_norm else y


def _ffn(x, g, w_up, conv_w, conv_b, w_down, g_final, final_norm, tm):
    b, s, d = x.shape
    f = w_down.shape[0]
    tm = min(tm, s)
    xf = x.reshape(b * s, d)
    tok = pl.BlockSpec((tm, d), lambda t: (t, 0))
    const = lambda arr: pl.BlockSpec(arr.shape, lambda t: (0,) * arr.ndim,
                                     pipeline_mode=pl.Buffered(1))
    weights = (g, w_up, conv_w, conv_b, w_down, g_final)
    out = pl.pallas_call(
        functools.partial(_ffn_kernel, tiles_per_seq=s // tm, final_norm=final_norm),
        grid=(b * s // tm,),
        in_specs=[tok] + [const(w) for w in weights],
        out_specs=tok,
        out_shape=jax.ShapeDtypeStruct((b * s, d), F32),
        scratch_shapes=[pltpu.VMEM((tm, f), BF16), pltpu.VMEM((SUBLANES, f), F32)],
        compiler_params=_cparams(("arbitrary",)),
        name="ffn",
    )(xf, *weights)
    return out.reshape(b, s, d)


def _sgu_kernel(x_ref, vec_ref, win_ref, ws_ref, bst_ref, expand_ref, wout_ref, out_ref,
                wc_sc, bias_sc, *, n_sub):
    d = x_ref.shape[-1]
    tm = x_ref.shape[0]
    C, G = SGU_CHUNK, SGU_GROUPS
    gd = d // G
    gpb = MXU_TILE // gd

    nblk = d // MXU_TILE

    @pl.when((pl.program_id(0) == 0) & (pl.program_id(1) == 0))
    def _():
        causal = _iota((C, C), 0) >= _iota((C, C), 1)
        for g in range(G):
            wc_sc[g // gpb, :, (g % gpb) * C:(g % gpb + 1) * C] = (
                jnp.where(causal, ws_ref[g], 0.0).astype(BF16))
        bias_sc[...] = _dot_exact_rhs(bst_ref[...], expand_ref[...])

    vec = vec_ref[...]
    gn, ln_g, ln_b = vec[0:1, :], vec[1:2, :], vec[2:3, :]
    lane_grp = _iota((C, MXU_TILE), 1) // gd
    bias = bias_sc[...]
    ts = tm // n_sub
    subs = range(n_sub)
    rsl = [slice(s * ts, (s + 1) * ts) for s in subs]
    x = [x_ref[rsl[s], :] for s in subs]
    z = [jax.nn.gelu(_dot(_rms(x[s], gn), win_ref[...])) for s in subs]
    vn = []
    for s in subs:
        v = z[s][:, d:]
        dv = v - jnp.mean(v, axis=-1, keepdims=True)
        var = jnp.mean(dv * dv, axis=-1, keepdims=True)
        vn.append((dv * lax.rsqrt(var + SGU_LN_EPS) * ln_g + ln_b).astype(BF16))
    gated = []
    for s in subs:
        chunks = []
        for c in range(ts // C):
            blocks = []
            for blk in range(nblk):
                vb = vn[s][c * C:(c + 1) * C, blk * MXU_TILE:(blk + 1) * MXU_TILE]
                rhs = jnp.concatenate(
                    [jnp.where(lane_grp == gi, vb, jnp.zeros_like(vb)) for gi in range(gpb)],
                    axis=0)
                blocks.append(jnp.dot(wc_sc[blk], rhs, preferred_element_type=F32))
            chunks.append(jnp.concatenate(blocks, axis=-1) + bias)
        gated.append((z[s][:, :d] * jnp.concatenate(chunks, axis=0)).astype(BF16))
    for s in subs:
        out_ref[rsl[s], :] = x[s] + jnp.dot(gated[s], wout_ref[...], preferred_element_type=F32)


def _sgu(x, vecs, w_in, w_s, b_s, w_out, tm, n_sub):
    b, s, d = x.shape
    tm = min(tm, s)
    gd = d // SGU_GROUPS
    gpb = MXU_TILE // gd
    expand = (jnp.arange(d)[None, :] // gd == jnp.arange(SGU_CHUNK)[:, None]).astype(BF16)
    bst = jnp.pad(b_s.T, ((0, 0), (0, SGU_CHUNK - SGU_GROUPS)))
    tok = pl.BlockSpec((None, tm, d), lambda bi, i: (bi, i, 0))
    full = lambda arr: pl.BlockSpec(arr.shape, lambda bi, i: (0,) * arr.ndim)
    ins = (vecs, w_in, w_s, bst, expand, w_out)
    return pl.pallas_call(
        functools.partial(_sgu_kernel, n_sub=n_sub),
        grid=(b, s // tm),
        in_specs=[tok] + [full(a) for a in ins],
        out_specs=tok,
        out_shape=jax.ShapeDtypeStruct((b, s, d), F32),
        scratch_shapes=[pltpu.VMEM((SGU_GROUPS // gpb, SGU_CHUNK, gpb * SGU_CHUNK), BF16),
                        pltpu.VMEM((SGU_CHUNK, d), F32)],
        compiler_params=_cparams(("arbitrary", "arbitrary")),
        name="sgu",
    )(x, *ins)


def _pad_rows(rows, d):
    n = -(-len(rows) // SUBLANES) * SUBLANES
    arr = jnp.stack([r.reshape(d).astype(F32) for r in rows])
    return jnp.pad(arr, ((0, n - len(rows)), (0, 0)))


def kernel(x, mem, norm_mix, norm_mem, norm_ffn, norm_final, mem_norm, rw_mu, rw_w_rkv, rw_w0, rw_w1, rw_w2, rw_a0, rw_a1, rw_a2, rw_g1, rw_g2, rw_k_k, rw_k_a, rw_r_k, rw_lnx_g, rw_lnx_b, rw_w_o, gm_w_in, gm_ln_g, gm_ln_b, gm_w_s, gm_b_s, gm_w_out, ca_w_q, ca_w_kv, ca_w_o, ff_w_up, ff_conv_w, ff_conv_b, ff_w_down):
    d = x.shape[-1]
    depth = norm_mix.shape[0]
    bf = lambda w: w.astype(BF16)
    lane = jnp.arange(MXU_TILE)
    ones_bd = (lane[:, None] // RWKV_HEAD_DIM == lane[None, :] // RWKV_HEAD_DIM).astype(BF16)
    kv = _mem_kv(mem, mem_norm, ca_w_kv)
    for i in range(depth):
        j = i // 2
        if i % 2 == 0:
            vecs = _pad_rows([norm_mix[i]] + [rw_mu[j, n] for n in range(6)]
                             + [rw_w0[j], rw_a0[j], rw_k_k[j], rw_k_a[j], rw_r_k[j]], d)
            r, lw, k, v, kk, b, g, bonus = _rwkv_proj(
                x, vecs, ones_bd, bf(rw_w_rkv[j]), bf(rw_w1[j]), bf(rw_w2[j]), bf(rw_a1[j]),
                bf(rw_a2[j]), bf(rw_g1[j]), bf(rw_g2[j]), tm=512, n_sub=2)
            o = _wkv(r, lw, k, v, kk, b, tc=128, lanes=1024)
            x = _rwkv_out(x, o, g, bonus, _pad_rows([rw_lnx_g[j], rw_lnx_b[j]], d), ones_bd,
                          bf(rw_w_o[j]), tm=512)
        else:
            x = _sgu(x, _pad_rows([norm_mix[i], gm_ln_g[j], gm_ln_b[j]], d), bf(gm_w_in[j]),
                     gm_w_s[j], gm_b_s[j], bf(gm_w_out[j]), tm=512, n_sub=2)
        x = _xattn(x, norm_mem[i].reshape(1, d), bf(ca_w_q[i]), kv[i], bf(ca_w_o[i]), tm=1024)
        x = _ffn(x, norm_ffn[i].reshape(1, d), bf(ff_w_up[i]), ff_conv_w[i],
                 ff_conv_b[i].reshape(1, -1), bf(ff_w_down[i]), norm_final.reshape(1, d),
                 final_norm=(i == depth - 1), tm=512)
    return x
```

```python
import functools
import math

import jax
import jax.numpy as jnp
from jax import lax
from jax.experimental import pallas as pl
from jax.experimental.pallas import tpu as pltpu

F32 = jnp.float32
BF16 = jnp.bfloat16

RWKV_HEAD_DIM = 64
RWKV_GN_EPS = 64e-5
SGU_CHUNK = 128
SGU_GROUPS = 16
SGU_LN_EPS = 1e-5
XATTN_HEADS = 4
RMS_EPS = 1e-6

MXU_TILE = 256
SUBLANES = 8
WKV_CHUNK = 64
WKV_PAIR = 2 * RWKV_HEAD_DIM
VMEM_LIMIT = 56 * 1024 * 1024


def _cparams(sem):
    return pltpu.CompilerParams(dimension_semantics=sem, vmem_limit_bytes=VMEM_LIMIT)


def _dot(a, b):
    return jnp.dot(a.astype(BF16), b.astype(BF16), preferred_element_type=F32)


def _dot_nt(a, b):
    return lax.dot_general(a.astype(BF16), b.astype(BF16), (((1,), (1,)), ((), ())),
                           preferred_element_type=F32)


def _dot_tn(a, b):
    return lax.dot_general(a.astype(BF16), b.astype(BF16), (((0,), (0,)), ((), ())),
                           preferred_element_type=F32)


def _split3(x):
    hi = x.astype(BF16)
    r1 = x - hi.astype(F32)
    mid = r1.astype(BF16)
    lo = (r1 - mid.astype(F32)).astype(BF16)
    return hi, mid, lo


def _dot_exact_lhs(a_bf16, x):
    return sum(jnp.dot(a_bf16, t, preferred_element_type=F32) for t in _split3(x))


def _dot_exact_rhs(x, b_bf16):
    return sum(jnp.dot(t, b_bf16, preferred_element_type=F32) for t in _split3(x))


def _rms(x, g):
    return x * lax.rsqrt(jnp.mean(x * x, axis=-1, keepdims=True) + RMS_EPS) * g


def _sigmoid(x):
    return 1.0 / (1.0 + jnp.exp(-x))


def _group_sum(x, ones_bd):
    d = x.shape[-1]
    parts = [_dot(x[:, j:j + MXU_TILE], ones_bd) for j in range(0, d, MXU_TILE)]
    return jnp.concatenate(parts, axis=-1)


def _iota(shape, dim):
    return lax.broadcasted_iota(jnp.int32, shape, dim)


def _mem_kv_kernel(mem_ref, g_ref, w_ref, o_ref):
    mn = _rms(mem_ref[...], g_ref[...])
    o_ref[...] = _dot(mn, w_ref[...]).astype(o_ref.dtype)


def _mem_kv(mem, mem_norm, w_kv):
    b, m, d = mem.shape
    nl = w_kv.shape[0]
    return pl.pallas_call(
        _mem_kv_kernel,
        grid=(nl, b),
        in_specs=[pl.BlockSpec((None, m, d), lambda l, i: (i, 0, 0)),
                  pl.BlockSpec((1, d), lambda l, i: (0, 0)),
                  pl.BlockSpec((None, d, 2 * d), lambda l, i: (l, 0, 0))],
        out_specs=pl.BlockSpec((None, None, m, 2 * d), lambda l, i: (l, i, 0, 0)),
        out_shape=jax.ShapeDtypeStruct((nl, b, m, 2 * d), BF16),
        compiler_params=_cparams(("arbitrary", "arbitrary")),
        name="mem_kv",
    )(mem, mem_norm.reshape(1, d), w_kv.astype(BF16))


def _rwkv_proj_kernel(x_ref, xp_ref, vec_ref, ones_ref, wrkv_ref, w1_ref, w2_ref, a1_ref, a2_ref,
                      g1_ref, g2_ref,
                      r_ref, lw_ref, k_ref, v_ref, kk_ref, b_ref, g_ref, bonus_ref, *, n_sub):
    i = pl.program_id(1)
    vec = vec_ref[...]
    row = lambda n: vec[n:n + 1, :]
    gn, w0, a0, k_k, k_a, r_k = row(0), row(7), row(8), row(9), row(10), row(11)
    ones_bd = ones_ref[...]
    ts = x_ref.shape[0] // n_sub
    subs = range(n_sub)
    rsl = [slice(u * ts, (u + 1) * ts) for u in subs]
    h = [_rms(x_ref[rsl[u], :], gn) for u in subs]
    hp = _rms(xp_ref[...], gn)[SUBLANES - 1:SUBLANES, :]
    hp = jnp.where(i == 0, 0.0, hp)
    first = _iota((ts, h[0].shape[1]), 0) == 0
    lasts = [hp] + [h[u][ts - 1:ts, :] for u in subs[:-1]]
    dx = [jnp.where(first, lasts[u], pltpu.roll(h[u], 1, axis=0)) - h[u] for u in subs]
    hb = [h[u].astype(BF16) for u in subs]
    dxb = [dx[u].astype(BF16) for u in subs]
    mix = lambda u, j: hb[u] + dxb[u] * row(1 + j).astype(BF16)
    r = [_dot(mix(u, 0), wrkv_ref[0]) for u in subs]
    k = [_dot(mix(u, 2), wrkv_ref[1]) for u in subs]
    v = [_dot(mix(u, 3), wrkv_ref[2]) for u in subs]
    wl = [_dot(jnp.tanh(_dot(mix(u, 1), w1_ref[...])), w2_ref[...]) for u in subs]
    al = [_dot(_dot(mix(u, 4), a1_ref[...]), a2_ref[...]) for u in subs]
    g = [_dot(_sigmoid(_dot(mix(u, 5), g1_ref[...])), g2_ref[...]) for u in subs]
    for u in subs:
        lw_ref[rsl[u], :] = -math.exp(-0.5) * _sigmoid(w0 + wl[u])
        a = _sigmoid(a0 + al[u])
        kk_raw = k[u] * k_k
        kk = kk_raw * lax.rsqrt(jnp.maximum(_group_sum(kk_raw * kk_raw, ones_bd), 1e-24))
        k2 = k[u] * (1.0 + (a - 1.0) * k_a)
        bonus = _group_sum(r[u] * k2 * r_k, ones_bd) * v[u]
        r_ref[rsl[u], :] = r[u].astype(r_ref.dtype)
        k_ref[rsl[u], :] = k2.astype(k_ref.dtype)
        v_ref[rsl[u], :] = v[u].astype(v_ref.dtype)
        kk_ref[rsl[u], :] = kk.astype(kk_ref.dtype)
        b_ref[rsl[u], :] = (kk * a).astype(b_ref.dtype)
        g_ref[rsl[u], :] = g[u].astype(g_ref.dtype)
        bonus_ref[rsl[u], :] = bonus.astype(bonus_ref.dtype)


def _rwkv_proj(x, vecs, ones_bd, w_rkv, w1, w2, a1, a2, g1, g2, tm, n_sub):
    b, s, d = x.shape
    tm = min(tm, s)
    tok = pl.BlockSpec((None, tm, d), lambda bi, i: (bi, i, 0))
    full = lambda arr: pl.BlockSpec(arr.shape, lambda bi, i: (0,) * arr.ndim,
                                    pipeline_mode=pl.Buffered(1))
    prev = pl.BlockSpec((None, SUBLANES, d),
                        lambda bi, i: (bi, jnp.maximum(i * (tm // SUBLANES) - 1, 0), 0))
    weights = (vecs, ones_bd, w_rkv, w1, w2, a1, a2, g1, g2)
    out_dt = (BF16, F32, BF16, BF16, BF16, BF16, BF16, BF16)
    return pl.pallas_call(
        functools.partial(_rwkv_proj_kernel, n_sub=n_sub),
        grid=(b, s // tm),
        in_specs=[tok, prev] + [full(w) for w in weights],
        out_specs=[tok] * 8,
        out_shape=[jax.ShapeDtypeStruct((b, s, d), dt) for dt in out_dt],
        compiler_params=_cparams(("arbitrary", "arbitrary")),
        name="rwkv_proj",
    )(x, x, *weights)


def _wkv_kernel(r_ref, lw_ref, k_ref, v_ref, kk_ref, b_ref, tri_ref, o_ref, st_ref,
                *, n_chunks, n_pairs):
    L, W = WKV_CHUNK, WKV_PAIR
    nh = W // RWKV_HEAD_DIM

    @pl.when(pl.program_id(2) == 0)
    def _():
        st_ref[...] = jnp.zeros_like(st_ref)

    ri = _iota((W, W), 0)
    ci = _iota((W, W), 1)
    strict = ri > ci
    diag = ri == ci
    incl2 = _iota((W, 2 * W), 0) >= (_iota((W, 2 * W), 1) % W)
    lane_head = _iota((L, W), 1) // RWKV_HEAD_DIM
    tri = tri_ref[...]

    def stack(x):
        return jnp.concatenate([jnp.where(lane_head == h, x, 0.0) for h in range(nh)],
                               axis=0).astype(BF16)

    units = [(c, p) for c in range(n_chunks) for p in range(n_pairs)]
    nu = len(units)
    at, rt, vs, bk, bkh, wlast = [], [], [], [], [], []
    for c in range(n_chunks):
        rows = slice(c * L, (c + 1) * L)
        lw = lw_ref[rows, :]
        cum = _dot_exact_lhs(tri, lw)
        cum_last = cum[L - 1:L, :]
        w_inv = jnp.exp(-cum)
        w_tail = jnp.exp(cum_last - cum)
        w_last = jnp.exp(cum_last)
        k = k_ref[rows, :].astype(F32)
        b = b_ref[rows, :].astype(F32)
        rt_c = r_ref[rows, :].astype(F32) * jnp.exp(cum)
        at_c = -kk_ref[rows, :].astype(F32) * jnp.exp(cum - lw)
        v_c = v_ref[rows, :].astype(F32)
        bt_c, kt_c, bh_c, kh_c = b * w_inv, k * w_inv, b * w_tail, k * w_tail
        for p in range(n_pairs):
            ln = slice(p * W, (p + 1) * W)
            rt.append(stack(rt_c[:, ln]))
            at.append(stack(at_c[:, ln]))
            vs.append(stack(v_c[:, ln]))
            bk.append(jnp.concatenate([stack(bt_c[:, ln]), stack(kt_c[:, ln])], axis=0))
            bkh.append(jnp.concatenate([stack(bh_c[:, ln]), stack(kh_c[:, ln])], axis=0))
            wlast.append(w_last[:, ln])
    a_all = [_dot_nt(at[u], bk[u]) for u in range(nu)]
    r_all = [_dot_nt(rt[u], bk[u]) for u in range(nu)]
    nil = [jnp.where(strict, a_all[u][:, :W], 0.0) for u in range(nu)]
    a_ak = [jnp.where(strict, a_all[u][:, W:], 0.0).astype(BF16) for u in range(nu)]
    a_r = [jnp.where(incl2, r_all[u], 0.0).astype(BF16) for u in range(nu)]
    av = [_dot(a_ak[u], vs[u]).astype(BF16) for u in range(nu)]
    t = [jnp.where(diag, 1.0, nil[u]) for u in range(nu)]
    pw = [nil[u].astype(BF16) for u in range(nu)]
    pw = [_dot(pw[u], pw[u]).astype(BF16) for u in range(nu)]
    for _ in range(4):
        both = [_dot(pw[u], jnp.concatenate([pw[u], t[u].astype(BF16)], axis=1))
                for u in range(nu)]
        pw = [both[u][:, :W].astype(BF16) for u in range(nu)]
        t = [t[u] + both[u][:, W:] for u in range(nu)]
    t = [t[u] + _dot(pw[u], t[u]) for u in range(nu)]
    pu = [_dot(t[u], jnp.concatenate([at[u], av[u]], axis=1)).astype(BF16)
          for u in range(nu)]
    zero = jnp.zeros((W, W), BF16)
    rhs = [jnp.concatenate([pu[u], jnp.concatenate([zero, vs[u]], axis=1)], axis=0)
           for u in range(nu)]
    qo = [_dot(a_r[u], rhs[u]) for u in range(nu)]
    gh = [_dot_tn(bkh[u], rhs[u]) for u in range(nu)]
    qg_lhs = [jnp.concatenate(
        [(rt[u].astype(F32) + qo[u][:, :W]).astype(BF16),
         (jnp.where(diag, wlast[u], 0.0) + gh[u][:, :W]).astype(BF16)], axis=0)
        for u in range(nu)]
    states = [st_ref[p] for p in range(n_pairs)]
    for u, (c, p) in enumerate(units):
        qg = _dot(qg_lhs[u], states[p])
        o_s = qg[:W] + qo[u][:, W:]
        o_ref[c * L:(c + 1) * L, p * W:(p + 1) * W] = (o_s[:L] + o_s[L:]).astype(o_ref.dtype)
        states[p] = qg[W:] + gh[u][:, W:]
    for p in range(n_pairs):
        st_ref[p] = states[p]


def _wkv(r, lw, k, v, kk, b, tc, lanes):
    bsz, s, d = r.shape
    tc = min(tc, s)
    n_chunks = tc // WKV_CHUNK
    W = min(lanes, d)
    n_pairs = W // WKV_PAIR
    tri = (jnp.arange(WKV_CHUNK)[:, None] >= jnp.arange(WKV_CHUNK)[None, :]).astype(BF16)
    blk = pl.BlockSpec((None, tc, W), lambda bi, q, i: (bi, i, q))
    return pl.pallas_call(
        functools.partial(_wkv_kernel, n_chunks=n_chunks, n_pairs=n_pairs),
        grid=(bsz, d // W, s // tc),
        in_specs=[blk] * 6 + [pl.BlockSpec(tri.shape, lambda bi, q, i: (0, 0))],
        out_specs=blk,
        out_shape=jax.ShapeDtypeStruct((bsz, s, d), BF16),
        scratch_shapes=[pltpu.VMEM((n_pairs, WKV_PAIR, WKV_PAIR), F32)],
        compiler_params=_cparams(("arbitrary", "arbitrary", "arbitrary")),
        name="wkv",
    )(r, lw, k, v, kk, b, tri)


def _rwkv_out_math(x, o, g, bonus, vec, ones_bd, w_o):
    lnx_g, lnx_b = vec[0:1, :], vec[1:2, :]
    o = o.astype(F32)
    inv_n = 1.0 / RWKV_HEAD_DIM
    mean = _group_sum(o, ones_bd) * inv_n
    dlt = o - mean
    var = _group_sum(dlt * dlt, ones_bd) * inv_n
    y = dlt * lax.rsqrt(var + RWKV_GN_EPS) * lnx_g + lnx_b + bonus.astype(F32)
    return x + _dot(y * g.astype(F32), w_o)


def _xattn_math(x, g, wq_ref, kv_ref, wo_ref):
    d = x.shape[-1]
    hd = d // XATTN_HEADS
    q = _dot(_rms(x, g), wq_ref[...])
    outs = []
    for h in range(XATTN_HEADS):
        kh = kv_ref[:, h * hd:(h + 1) * hd]
        vh = kv_ref[:, d + h * hd:d + (h + 1) * hd]
        s = _dot_nt(q[:, h * hd:(h + 1) * hd], kh) * (hd ** -0.5)
        p = jnp.exp(s - jnp.max(s, axis=-1, keepdims=True))
        p = p * (1.0 / jnp.sum(p, axis=-1, keepdims=True))
        outs.append(_dot(p, vh).astype(BF16))
    return x + _dot(jnp.concatenate(outs, axis=-1), wo_ref[...])


def _xattn_kernel(x_ref, g_ref, wq_ref, kv_ref, wo_ref, out_ref):
    out_ref[...] = _xattn_math(x_ref[...], g_ref[...], wq_ref, kv_ref, wo_ref)


def _rwkv_out_xattn_kernel(x_ref, o_ref, g_ref, bonus_ref, vec_ref, ones_ref, wo1_ref,
                           gx_ref, wq_ref, kv_ref, wo_ref, out_ref):
    x = _rwkv_out_math(x_ref[...], o_ref[...], g_ref[...], bonus_ref[...], vec_ref[...],
                       ones_ref[...], wo1_ref[...])
    out_ref[...] = _xattn_math(x, gx_ref[...], wq_ref, kv_ref, wo_ref)


def _rwkv_out_xattn(x, o, g, bonus, vecs, ones_bd, w_o1, gx, w_q, kv, w_o, tm):
    b, s, d = x.shape
    tm = min(tm, s)
    m = kv.shape[1]
    tok = pl.BlockSpec((None, tm, d), lambda bi, i: (bi, i, 0))
    full = lambda arr: pl.BlockSpec(arr.shape, lambda bi, i: (0,) * arr.ndim,
                                    pipeline_mode=pl.Buffered(1))
    return pl.pallas_call(
        _rwkv_out_xattn_kernel,
        grid=(b, s // tm),
        in_specs=[tok, tok, tok, tok, full(vecs), full(ones_bd), full(w_o1), full(gx), full(w_q),
                  pl.BlockSpec((None, m, 2 * d), lambda bi, i: (bi, 0, 0)), full(w_o)],
        out_specs=tok,
        out_shape=jax.ShapeDtypeStruct((b, s, d), F32),
        compiler_params=_cparams(("arbitrary", "arbitrary")),
        name="rwkv_out_xattn",
    )(x, o, g, bonus, vecs, ones_bd, w_o1, gx, w_q, kv, w_o)


def _xattn(x, g, w_q, kv, w_o, tm):
    b, s, d = x.shape
    tm = min(tm, s)
    m = kv.shape[1]
    tok = pl.BlockSpec((None, tm, d), lambda bi, i: (bi, i, 0))
    full = lambda arr: pl.BlockSpec(arr.shape, lambda bi, i: (0,) * arr.ndim)
    return pl.pallas_call(
        _xattn_kernel,
        grid=(b, s // tm),
        in_specs=[tok, full(g), full(w_q),
                  pl.BlockSpec((None, m, 2 * d), lambda bi, i: (bi, 0, 0)), full(w_o)],
        out_specs=tok,
        out_shape=jax.ShapeDtypeStruct((b, s, d), F32),
        compiler_params=_cparams(("arbitrary", "arbitrary")),
        name="xattn",
    )(x, g, w_q, kv, w_o)


def _ffn_kernel(x_ref, g_ref, wup_ref, cw_ref, cb_ref, wd_ref, gf_ref, out_ref,
                act_sc, carry_sc, *, tiles_per_seq, final_norm):
    tm = x_ref.shape[0]
    f = wd_ref.shape[0]
    fc = MXU_TILE

    @pl.when((pl.program_id(0) % tiles_per_seq) == 0)
    def _():
        carry_sc[...] = jnp.zeros_like(carry_sc)

    x = x_ref[...]
    xn = _rms(x, g_ref[...]).astype(BF16)
    rows = _iota((tm, fc), 0)
    for j in range(f // fc):
        cols = slice(j * fc, (j + 1) * fc)
        gate = jnp.dot(xn, wup_ref[:, cols], preferred_element_type=F32)
        val = jnp.dot(xn, wup_ref[:, f + j * fc:f + (j + 1) * fc], preferred_element_type=F32)
        prev = carry_sc[:, cols]
        carry_sc[:, cols] = gate[tm - SUBLANES:tm, :]
        s1 = jnp.where(rows == 0, prev[SUBLANES - 1:SUBLANES, :], pltpu.roll(gate, 1, axis=0))
        s2 = pltpu.roll(gate, 2, axis=0)
        s2 = jnp.where(rows == 0, prev[SUBLANES - 2:SUBLANES - 1, :], s2)
        s2 = jnp.where(rows == 1, prev[SUBLANES - 1:SUBLANES, :], s2)
        conv = (s2 * cw_ref[0:1, cols] + s1 * cw_ref[1:2, cols] + gate * cw_ref[2:3, cols]
                + cb_ref[:, cols])
        act_sc[:, cols] = (conv * _sigmoid(conv) * val).astype(BF16)
    y = x + jnp.dot(act_sc[...], wd_ref[...], preferred_element_type=F32)
    out_ref[...] = _rms(y, gf_ref[...]) if final_norm else y


def _ffn(x, g, w_up, conv_w, conv_b, w_down, g_final, final_norm, tm):
    b, s, d = x.shape
    f = w_down.shape[0]
    tm = min(tm, s)
    xf = x.reshape(b * s, d)
    tok = pl.BlockSpec((tm, d), lambda t: (t, 0))
    const = lambda arr: pl.BlockSpec(arr.shape, lambda t: (0,) * arr.ndim,
                                     pipeline_mode=pl.Buffered(1))
    weights = (g, w_up, conv_w, conv_b, w_down, g_final)
    out = pl.pallas_call(
        functools.partial(_ffn_kernel, tiles_per_seq=s // tm, final_norm=final_norm),
        grid=(b * s // tm,),
        in_specs=[tok] + [const(w) for w in weights],
        out_specs=tok,
        out_shape=jax.ShapeDtypeStruct((b * s, d), F32),
        scratch_shapes=[pltpu.VMEM((tm, f), BF16), pltpu.VMEM((SUBLANES, f), F32)],
        compiler_params=_cparams(("arbitrary",)),
        name="ffn",
    )(xf, *weights)
    return out.reshape(b, s, d)


def _sgu_kernel(x_ref, vec_ref, win_ref, ws_ref, bst_ref, expand_ref, wout_ref, out_ref,
                wc_sc, bias_sc, *, n_sub):
    d = x_ref.shape[-1]
    tm = x_ref.shape[0]
    C, G = SGU_CHUNK, SGU_GROUPS
    gd = d // G
    gpb = MXU_TILE // gd

    nblk = d // MXU_TILE

    @pl.when((pl.program_id(0) == 0) & (pl.program_id(1) == 0))
    def _():
        causal = _iota((C, C), 0) >= _iota((C, C), 1)
        for g in range(G):
            wc_sc[g // gpb, :, (g % gpb) * C:(g % gpb + 1) * C] = (
                jnp.where(causal, ws_ref[g], 0.0).astype(BF16))
        bias_sc[...] = _dot_exact_rhs(bst_ref[...], expand_ref[...])

    vec = vec_ref[...]
    gn, ln_g, ln_b = vec[0:1, :], vec[1:2, :], vec[2:3, :]
    lane_grp = _iota((C, MXU_TILE), 1) // gd
    bias = bias_sc[...]
    ts = tm // n_sub
    subs = range(n_sub)
    rsl = [slice(s * ts, (s + 1) * ts) for s in subs]
    x = [x_ref[rsl[s], :] for s in subs]
    z = [jax.nn.gelu(_dot(_rms(x[s], gn), win_ref[...])) for s in subs]
    vn = []
    for s in subs:
        v = z[s][:, d:]
        dv = v - jnp.mean(v, axis=-1, keepdims=True)
        var = jnp.mean(dv * dv, axis=-1, keepdims=True)
        vn.append((dv * lax.rsqrt(var + SGU_LN_EPS) * ln_g + ln_b).astype(BF16))
    gated = []
    for s in subs:
        chunks = []
        for c in range(ts // C):
            blocks = []
            for blk in range(nblk):
                vb = vn[s][c * C:(c + 1) * C, blk * MXU_TILE:(blk + 1) * MXU_TILE]
                rhs = jnp.concatenate(
                    [jnp.where(lane_grp == gi, vb, jnp.zeros_like(vb)) for gi in range(gpb)],
                    axis=0)
                blocks.append(jnp.dot(wc_sc[blk], rhs, preferred_element_type=F32))
            chunks.append(jnp.concatenate(blocks, axis=-1) + bias)
        gated.append((z[s][:, :d] * jnp.concatenate(chunks, axis=0)).astype(BF16))
    for s in subs:
        out_ref[rsl[s], :] = x[s] + jnp.dot(gated[s], wout_ref[...], preferred_element_type=F32)


def _sgu(x, vecs, w_in, w_s, b_s, w_out, tm, n_sub):
    b, s, d = x.shape
    tm = min(tm, s)
    gd = d // SGU_GROUPS
    gpb = MXU_TILE // gd
    expand = (jnp.arange(d)[None, :] // gd == jnp.arange(SGU_CHUNK)[:, None]).astype(BF16)
    bst = jnp.pad(b_s.T, ((0, 0), (0, SGU_CHUNK - SGU_GROUPS)))
    tok = pl.BlockSpec((None, tm, d), lambda bi, i: (bi, i, 0))
    full = lambda arr: pl.BlockSpec(arr.shape, lambda bi, i: (0,) * arr.ndim)
    ins = (vecs, w_in, w_s, bst, expand, w_out)
    return pl.pallas_call(
        functools.partial(_sgu_kernel, n_sub=n_sub),
        grid=(b, s // tm),
        in_specs=[tok] + [full(a) for a in ins],
        out_specs=tok,
        out_shape=jax.ShapeDtypeStruct((b, s, d), F32),
        scratch_shapes=[pltpu.VMEM((SGU_GROUPS // gpb, SGU_CHUNK, gpb * SGU_CHUNK), BF16),
                        pltpu.VMEM((SGU_CHUNK, d), F32)],
        compiler_params=_cparams(("arbitrary", "arbitrary")),
        name="sgu",
    )(x, *ins)


def _pad_rows(rows, d):
    n = -(-len(rows) // SUBLANES) * SUBLANES
    arr = jnp.stack([r.reshape(d).astype(F32) for r in rows])
    return jnp.pad(arr, ((0, n - len(rows)), (0, 0)))


def kernel(x, mem, norm_mix, norm_mem, norm_ffn, norm_final, mem_norm, rw_mu, rw_w_rkv, rw_w0, rw_w1, rw_w2, rw_a0, rw_a1, rw_a2, rw_g1, rw_g2, rw_k_k, rw_k_a, rw_r_k, rw_lnx_g, rw_lnx_b, rw_w_o, gm_w_in, gm_ln_g, gm_ln_b, gm_w_s, gm_b_s, gm_w_out, ca_w_q, ca_w_kv, ca_w_o, ff_w_up, ff_conv_w, ff_conv_b, ff_w_down):
    d = x.shape[-1]
    depth = norm_mix.shape[0]
    bf = lambda w: w.astype(BF16)
    lane = jnp.arange(MXU_TILE)
    ones_bd = (lane[:, None] // RWKV_HEAD_DIM == lane[None, :] // RWKV_HEAD_DIM).astype(BF16)
    kv = _mem_kv(mem, mem_norm, ca_w_kv)
    for i in range(depth):
        j = i // 2
        if i % 2 == 0:
            vecs = _pad_rows([norm_mix[i]] + [rw_mu[j, n] for n in range(6)]
                             + [rw_w0[j], rw_a0[j], rw_k_k[j], rw_k_a[j], rw_r_k[j]], d)
            r, lw, k, v, kk, b, g, bonus = _rwkv_proj(
                x, vecs, ones_bd, bf(rw_w_rkv[j]), bf(rw_w1[j]), bf(rw_w2[j]), bf(rw_a1[j]),
                bf(rw_a2[j]), bf(rw_g1[j]), bf(rw_g2[j]), tm=512, n_sub=2)
            o = _wkv(r, lw, k, v, kk, b, tc=256, lanes=1024)
            x = _rwkv_out_xattn(x, o, g, bonus, _pad_rows([rw_lnx_g[j], rw_lnx_b[j]], d),
                                ones_bd, bf(rw_w_o[j]), norm_mem[i].reshape(1, d),
                                bf(ca_w_q[i]), kv[i], bf(ca_w_o[i]), tm=1024)
        else:
            x = _sgu(x, _pad_rows([norm_mix[i], gm_ln_g[j], gm_ln_b[j]], d), bf(gm_w_in[j]),
                     gm_w_s[j], gm_b_s[j], bf(gm_w_out[j]), tm=512, n_sub=2)
            x = _xattn(x, norm_mem[i].reshape(1, d), bf(ca_w_q[i]), kv[i], bf(ca_w_o[i]),
                       tm=1024)
        x = _ffn(x, norm_ffn[i].reshape(1, d), bf(ff_w_up[i]), ff_conv_w[i],
                 ff_conv_b[i].reshape(1, -1), bf(ff_w_down[i]), norm_final.reshape(1, d),
                 final_norm=(i == depth - 1), tm=512)
    return x
```

```python
import functools
import math

import jax
import jax.numpy as jnp
from jax import lax
from jax.experimental import pallas as pl
from jax.experimental.pallas import tpu as pltpu

F32 = jnp.float32
BF16 = jnp.bfloat16

RWKV_HEAD_DIM = 64
RWKV_GN_EPS = 64e-5
SGU_CHUNK = 128
SGU_GROUPS = 16
SGU_LN_EPS = 1e-5
XATTN_HEADS = 4
RMS_EPS = 1e-6

MXU_TILE = 256
SUBLANES = 8
WKV_CHUNK = 64
WKV_PAIR = 2 * RWKV_HEAD_DIM
VMEM_LIMIT = 56 * 1024 * 1024


def _cparams(sem):
    return pltpu.CompilerParams(dimension_semantics=sem, vmem_limit_bytes=VMEM_LIMIT)


def _dot(a, b):
    return jnp.dot(a.astype(BF16), b.astype(BF16), preferred_element_type=F32)


def _dot_nt(a, b):
    return lax.dot_general(a.astype(BF16), b.astype(BF16), (((1,), (1,)), ((), ())),
                           preferred_element_type=F32)


def _dot_tn(a, b):
    return lax.dot_general(a.astype(BF16), b.astype(BF16), (((0,), (0,)), ((), ())),
                           preferred_element_type=F32)


def _split3(x):
    hi = x.astype(BF16)
    r1 = x - hi.astype(F32)
    mid = r1.astype(BF16)
    lo = (r1 - mid.astype(F32)).astype(BF16)
    return hi, mid, lo


def _dot_exact_lhs(a_bf16, x):
    return sum(jnp.dot(a_bf16, t, preferred_element_type=F32) for t in _split3(x))


def _dot_exact_rhs(x, b_bf16):
    return sum(jnp.dot(t, b_bf16, preferred_element_type=F32) for t in _split3(x))


def _rms(x, g):
    return x * lax.rsqrt(jnp.mean(x * x, axis=-1, keepdims=True) + RMS_EPS) * g


def _sigmoid(x):
    return 1.0 / (1.0 + jnp.exp(-x))


def _group_sum(x, ones_bd):
    d = x.shape[-1]
    parts = [_dot(x[:, j:j + MXU_TILE], ones_bd) for j in range(0, d, MXU_TILE)]
    return jnp.concatenate(parts, axis=-1)


def _iota(shape, dim):
    return lax.broadcasted_iota(jnp.int32, shape, dim)


def _mem_kv_kernel(mem_ref, g_ref, w_ref, o_ref):
    mn = _rms(mem_ref[...], g_ref[...])
    o_ref[...] = _dot(mn, w_ref[...]).astype(o_ref.dtype)


def _mem_kv(mem, mem_norm, w_kv):
    b, m, d = mem.shape
    nl = w_kv.shape[0]
    return pl.pallas_call(
        _mem_kv_kernel,
        grid=(nl, b),
        in_specs=[pl.BlockSpec((None, m, d), lambda l, i: (i, 0, 0)),
                  pl.BlockSpec((1, d), lambda l, i: (0, 0)),
                  pl.BlockSpec((None, d, 2 * d), lambda l, i: (l, 0, 0))],
        out_specs=pl.BlockSpec((None, None, m, 2 * d), lambda l, i: (l, i, 0, 0)),
        out_shape=jax.ShapeDtypeStruct((nl, b, m, 2 * d), BF16),
        compiler_params=_cparams(("arbitrary", "arbitrary")),
        name="mem_kv",
    )(mem, mem_norm.reshape(1, d), w_kv.astype(BF16))


def _rwkv_proj_kernel(x_ref, xp_ref, vec_ref, ones_ref, wrkv_ref, w1_ref, w2_ref, a1_ref, a2_ref,
                      g1_ref, g2_ref,
                      r_ref, lw_ref, k_ref, v_ref, kk_ref, b_ref, g_ref, bonus_ref, *, n_sub):
    i = pl.program_id(1)
    vec = vec_ref[...]
    row = lambda n: vec[n:n + 1, :]
    gn, w0, a0, k_k, k_a, r_k = row(0), row(7), row(8), row(9), row(10), row(11)
    ones_bd = ones_ref[...]
    ts = x_ref.shape[0] // n_sub
    subs = range(n_sub)
    rsl = [slice(u * ts, (u + 1) * ts) for u in subs]
    h = [_rms(x_ref[rsl[u], :], gn) for u in subs]
    hp = _rms(xp_ref[...], gn)[SUBLANES - 1:SUBLANES, :]
    hp = jnp.where(i == 0, 0.0, hp)
    first = _iota((ts, h[0].shape[1]), 0) == 0
    lasts = [hp] + [h[u][ts - 1:ts, :] for u in subs[:-1]]
    dx = [jnp.where(first, lasts[u], pltpu.roll(h[u], 1, axis=0)) - h[u] for u in subs]
    hb = [h[u].astype(BF16) for u in subs]
    dxb = [dx[u].astype(BF16) for u in subs]
    mix = lambda u, j: hb[u] + dxb[u] * row(1 + j).astype(BF16)

    def post_steps(u, raw):
        st = {}

        def s0():
            lw_ref[rsl[u], :] = -math.exp(-0.5) * _sigmoid(w0 + raw["wl"])
            st["a"] = _sigmoid(a0 + raw["al"])
            g_ref[rsl[u], :] = raw["g"].astype(g_ref.dtype)

        def s1():
            kk_raw = raw["k"] * k_k
            kk = kk_raw * lax.rsqrt(jnp.maximum(_group_sum(kk_raw * kk_raw, ones_bd), 1e-24))
            kk_ref[rsl[u], :] = kk.astype(kk_ref.dtype)
            b_ref[rsl[u], :] = (kk * st["a"]).astype(b_ref.dtype)

        def s2():
            k2 = raw["k"] * (1.0 + (st["a"] - 1.0) * k_a)
            bonus = _group_sum(raw["r"] * k2 * r_k, ones_bd) * raw["v"]
            k_ref[rsl[u], :] = k2.astype(k_ref.dtype)
            bonus_ref[rsl[u], :] = bonus.astype(bonus_ref.dtype)
            r_ref[rsl[u], :] = raw["r"].astype(r_ref.dtype)
            v_ref[rsl[u], :] = raw["v"].astype(v_ref.dtype)

        return [s0, s1, s2]

    pending = []
    for u in subs:
        step = lambda: pending.pop(0)() if pending else None
        raw = {}
        raw["wl"] = _dot(jnp.tanh(_dot(mix(u, 1), w1_ref[...])), w2_ref[...])
        raw["al"] = _dot(_dot(mix(u, 4), a1_ref[...]), a2_ref[...])
        raw["g"] = _dot(_sigmoid(_dot(mix(u, 5), g1_ref[...])), g2_ref[...])
        raw["k"] = _dot(mix(u, 2), wrkv_ref[1])
        step()
        raw["r"] = _dot(mix(u, 0), wrkv_ref[0])
        step()
        raw["v"] = _dot(mix(u, 3), wrkv_ref[2])
        step()
        pending = post_steps(u, raw)
    for fn in pending:
        fn()


def _rwkv_proj(x, vecs, ones_bd, w_rkv, w1, w2, a1, a2, g1, g2, tm, n_sub):
    b, s, d = x.shape
    tm = min(tm, s)
    tok = pl.BlockSpec((None, tm, d), lambda bi, i: (bi, i, 0))
    full = lambda arr: pl.BlockSpec(arr.shape, lambda bi, i: (0,) * arr.ndim,
                                    pipeline_mode=pl.Buffered(1))
    prev = pl.BlockSpec((None, SUBLANES, d),
                        lambda bi, i: (bi, jnp.maximum(i * (tm // SUBLANES) - 1, 0), 0))
    weights = (vecs, ones_bd, w_rkv, w1, w2, a1, a2, g1, g2)
    out_dt = (BF16, F32, BF16, BF16, BF16, BF16, BF16, BF16)
    return pl.pallas_call(
        functools.partial(_rwkv_proj_kernel, n_sub=n_sub),
        grid=(b, s // tm),
        in_specs=[tok, prev] + [full(w) for w in weights],
        out_specs=[tok] * 8,
        out_shape=[jax.ShapeDtypeStruct((b, s, d), dt) for dt in out_dt],
        compiler_params=_cparams(("arbitrary", "arbitrary")),
        name="rwkv_proj",
    )(x, x, *weights)


def _wkv_kernel(r_ref, lw_ref, k_ref, v_ref, kk_ref, b_ref, tri_ref, o_ref, st_ref,
                *, n_chunks, n_pairs):
    L, W = WKV_CHUNK, WKV_PAIR
    nh = W // RWKV_HEAD_DIM

    @pl.when(pl.program_id(2) == 0)
    def _():
        st_ref[...] = jnp.zeros_like(st_ref)

    ri = _iota((W, W), 0)
    ci = _iota((W, W), 1)
    strict = ri > ci
    diag = ri == ci
    incl2 = _iota((W, 2 * W), 0) >= (_iota((W, 2 * W), 1) % W)
    lane_head = _iota((L, W), 1) // RWKV_HEAD_DIM
    tri = tri_ref[...]

    def stack(x):
        return jnp.concatenate([jnp.where(lane_head == h, x, 0.0) for h in range(nh)],
                               axis=0).astype(BF16)

    units = [(c, p) for c in range(n_chunks) for p in range(n_pairs)]
    nu = len(units)
    at, rt, vs, bk, bkh, wlast = [], [], [], [], [], []
    for c in range(n_chunks):
        rows = slice(c * L, (c + 1) * L)
        lw = lw_ref[rows, :]
        cum = _dot_exact_lhs(tri, lw)
        cum_last = cum[L - 1:L, :]
        w_inv = jnp.exp(-cum)
        w_tail = jnp.exp(cum_last - cum)
        w_last = jnp.exp(cum_last)
        k = k_ref[rows, :].astype(F32)
        b = b_ref[rows, :].astype(F32)
        rt_c = r_ref[rows, :].astype(F32) * jnp.exp(cum)
        at_c = -kk_ref[rows, :].astype(F32) * jnp.exp(cum - lw)
        v_c = v_ref[rows, :].astype(F32)
        bt_c, kt_c, bh_c, kh_c = b * w_inv, k * w_inv, b * w_tail, k * w_tail
        for p in range(n_pairs):
            ln = slice(p * W, (p + 1) * W)
            rt.append(stack(rt_c[:, ln]))
            at.append(stack(at_c[:, ln]))
            vs.append(stack(v_c[:, ln]))
            bk.append(jnp.concatenate([stack(bt_c[:, ln]), stack(kt_c[:, ln])], axis=0))
            bkh.append(jnp.concatenate([stack(bh_c[:, ln]), stack(kh_c[:, ln])], axis=0))
            wlast.append(w_last[:, ln])
    a_all = [_dot_nt(at[u], bk[u]) for u in range(nu)]
    r_all = [_dot_nt(rt[u], bk[u]) for u in range(nu)]
    nil = [jnp.where(strict, a_all[u][:, :W], 0.0) for u in range(nu)]
    a_ak = [jnp.where(strict, a_all[u][:, W:], 0.0).astype(BF16) for u in range(nu)]
    a_r = [jnp.where(incl2, r_all[u], 0.0).astype(BF16) for u in range(nu)]
    av = [_dot(a_ak[u], vs[u]).astype(BF16) for u in range(nu)]
    t = [jnp.where(diag, 1.0, nil[u]) for u in range(nu)]
    pw = [nil[u].astype(BF16) for u in range(nu)]
    pw = [_dot(pw[u], pw[u]).astype(BF16) for u in range(nu)]
    for _ in range(4):
        both = [_dot(pw[u], jnp.concatenate([pw[u], t[u].astype(BF16)], axis=1))
                for u in range(nu)]
        pw = [both[u][:, :W].astype(BF16) for u in range(nu)]
        t = [t[u] + both[u][:, W:] for u in range(nu)]
    t = [t[u] + _dot(pw[u], t[u]) for u in range(nu)]
    pu = [_dot(t[u], jnp.concatenate([at[u], av[u]], axis=1)).astype(BF16)
          for u in range(nu)]
    zero = jnp.zeros((W, W), BF16)
    rhs = [jnp.concatenate([pu[u], jnp.concatenate([zero, vs[u]], axis=1)], axis=0)
           for u in range(nu)]
    qo = [_dot(a_r[u], rhs[u]) for u in range(nu)]
    gh = [_dot_tn(bkh[u], rhs[u]) for u in range(nu)]
    qg_lhs = [jnp.concatenate(
        [(rt[u].astype(F32) + qo[u][:, :W]).astype(BF16),
         (jnp.where(diag, wlast[u], 0.0) + gh[u][:, :W]).astype(BF16)], axis=0)
        for u in range(nu)]
    states = [st_ref[p] for p in range(n_pairs)]
    for u, (c, p) in enumerate(units):
        qg = _dot(qg_lhs[u], states[p])
        o_s = qg[:W] + qo[u][:, W:]
        o_ref[c * L:(c + 1) * L, p * W:(p + 1) * W] = (o_s[:L] + o_s[L:]).astype(o_ref.dtype)
        states[p] = qg[W:] + gh[u][:, W:]
    for p in range(n_pairs):
        st_ref[p] = states[p]


def _wkv(r, lw, k, v, kk, b, tc, lanes):
    bsz, s, d = r.shape
    tc = min(tc, s)
    n_chunks = tc // WKV_CHUNK
    W = min(lanes, d)
    n_pairs = W // WKV_PAIR
    tri = (jnp.arange(WKV_CHUNK)[:, None] >= jnp.arange(WKV_CHUNK)[None, :]).astype(BF16)
    blk = pl.BlockSpec((None, tc, W), lambda bi, q, i: (bi, i, q))
    return pl.pallas_call(
        functools.partial(_wkv_kernel, n_chunks=n_chunks, n_pairs=n_pairs),
        grid=(bsz, d // W, s // tc),
        in_specs=[blk] * 6 + [pl.BlockSpec(tri.shape, lambda bi, q, i: (0, 0))],
        out_specs=blk,
        out_shape=jax.ShapeDtypeStruct((bsz, s, d), BF16),
        scratch_shapes=[pltpu.VMEM((n_pairs, WKV_PAIR, WKV_PAIR), F32)],
        compiler_params=_cparams(("arbitrary", "arbitrary", "arbitrary")),
        name="wkv",
    )(r, lw, k, v, kk, b, tri)


def _rwkv_out_math(x, o, g, bonus, vec, ones_bd, w_o):
    lnx_g, lnx_b = vec[0:1, :], vec[1:2, :]
    o = o.astype(F32)
    inv_n = 1.0 / RWKV_HEAD_DIM
    mean = _group_sum(o, ones_bd) * inv_n
    dlt = o - mean
    var = _group_sum(dlt * dlt, ones_bd) * inv_n
    y = dlt * lax.rsqrt(var + RWKV_GN_EPS) * lnx_g + lnx_b + bonus.astype(F32)
    return x + _dot(y * g.astype(F32), w_o)


def _xattn_math(x, g, wq_ref, kv_ref, wo_ref):
    d = x.shape[-1]
    hd = d // XATTN_HEADS
    q = _dot(_rms(x, g), wq_ref[...])
    outs = []
    for h in range(XATTN_HEADS):
        kh = kv_ref[:, h * hd:(h + 1) * hd]
        vh = kv_ref[:, d + h * hd:d + (h + 1) * hd]
        s = _dot_nt(q[:, h * hd:(h + 1) * hd], kh) * (hd ** -0.5)
        p = jnp.exp(s - jnp.max(s, axis=-1, keepdims=True))
        p = p * (1.0 / jnp.sum(p, axis=-1, keepdims=True))
        outs.append(_dot(p, vh).astype(BF16))
    return x + _dot(jnp.concatenate(outs, axis=-1), wo_ref[...])


def _xattn_kernel(x_ref, g_ref, wq_ref, kv_ref, wo_ref, out_ref):
    out_ref[...] = _xattn_math(x_ref[...], g_ref[...], wq_ref, kv_ref, wo_ref)


def _rwkv_out_xattn_kernel(x_ref, o_ref, g_ref, bonus_ref, vec_ref, ones_ref, wo1_ref,
                           gx_ref, wq_ref, kv_ref, wo_ref, out_ref):
    x = _rwkv_out_math(x_ref[...], o_ref[...], g_ref[...], bonus_ref[...], vec_ref[...],
                       ones_ref[...], wo1_ref[...])
    out_ref[...] = _xattn_math(x, gx_ref[...], wq_ref, kv_ref, wo_ref)


def _rwkv_out_xattn(x, o, g, bonus, vecs, ones_bd, w_o1, gx, w_q, kv, w_o, tm):
    b, s, d = x.shape
    tm = min(tm, s)
    m = kv.shape[1]
    tok = pl.BlockSpec((None, tm, d), lambda bi, i: (bi, i, 0))
    full = lambda arr: pl.BlockSpec(arr.shape, lambda bi, i: (0,) * arr.ndim,
                                    pipeline_mode=pl.Buffered(1))
    return pl.pallas_call(
        _rwkv_out_xattn_kernel,
        grid=(b, s // tm),
        in_specs=[tok, tok, tok, tok, full(vecs), full(ones_bd), full(w_o1), full(gx), full(w_q),
                  pl.BlockSpec((None, m, 2 * d), lambda bi, i: (bi, 0, 0)), full(w_o)],
        out_specs=tok,
        out_shape=jax.ShapeDtypeStruct((b, s, d), F32),
        compiler_params=_cparams(("arbitrary", "arbitrary")),
        name="rwkv_out_xattn",
    )(x, o, g, bonus, vecs, ones_bd, w_o1, gx, w_q, kv, w_o)


def _xattn(x, g, w_q, kv, w_o, tm):
    b, s, d = x.shape
    tm = min(tm, s)
    m = kv.shape[1]
    tok = pl.BlockSpec((None, tm, d), lambda bi, i: (bi, i, 0))
    full = lambda arr: pl.BlockSpec(arr.shape, lambda bi, i: (0,) * arr.ndim)
    return pl.pallas_call(
        _xattn_kernel,
        grid=(b, s // tm),
        in_specs=[tok, full(g), full(w_q),
                  pl.BlockSpec((None, m, 2 * d), lambda bi, i: (bi, 0, 0)), full(w_o)],
        out_specs=tok,
        out_shape=jax.ShapeDtypeStruct((b, s, d), F32),
        compiler_params=_cparams(("arbitrary", "arbitrary")),
        name="xattn",
    )(x, g, w_q, kv, w_o)


def _ffn_kernel(x_ref, g_ref, wup_ref, cw_ref, cb_ref, wd_ref, gf_ref, out_ref,
                act_sc, carry_sc, *, tiles_per_seq, final_norm):
    tm = x_ref.shape[0]
    f = wd_ref.shape[0]
    fc = MXU_TILE

    @pl.when((pl.program_id(0) % tiles_per_seq) == 0)
    def _():
        carry_sc[...] = jnp.zeros_like(carry_sc)

    x = x_ref[...]
    xn = _rms(x, g_ref[...]).astype(BF16)
    rows = _iota((tm, fc), 0)
    for j in range(f // fc):
        cols = slice(j * fc, (j + 1) * fc)
        gate = jnp.dot(xn, wup_ref[:, cols], preferred_element_type=F32)
        val = jnp.dot(xn, wup_ref[:, f + j * fc:f + (j + 1) * fc], preferred_element_type=F32)
        prev = carry_sc[:, cols]
        carry_sc[:, cols] = gate[tm - SUBLANES:tm, :]
        s1 = jnp.where(rows == 0, prev[SUBLANES - 1:SUBLANES, :], pltpu.roll(gate, 1, axis=0))
        s2 = pltpu.roll(gate, 2, axis=0)
        s2 = jnp.where(rows == 0, prev[SUBLANES - 2:SUBLANES - 1, :], s2)
        s2 = jnp.where(rows == 1, prev[SUBLANES - 1:SUBLANES, :], s2)
        conv = (s2 * cw_ref[0:1, cols] + s1 * cw_ref[1:2, cols] + gate * cw_ref[2:3, cols]
                + cb_ref[:, cols])
        act_sc[:, cols] = (conv * _sigmoid(conv) * val).astype(BF16)
    y = x + jnp.dot(act_sc[...], wd_ref[...], preferred_element_type=F32)
    out_ref[...] = _rms(y, gf_ref[...]) if final_norm else y


def _ffn(x, g, w_up, conv_w, conv_b, w_down, g_final, final_norm, tm):
    b, s, d = x.shape
    f = w_down.shape[0]
    tm = min(tm, s)
    xf = x.reshape(b * s, d)
    tok = pl.BlockSpec((tm, d), lambda t: (t, 0))
    const = lambda arr: pl.BlockSpec(arr.shape, lambda t: (0,) * arr.ndim,
                                     pipeline_mode=pl.Buffered(1))
    weights = (g, w_up, conv_w, conv_b, w_down, g_final)
    out = pl.pallas_call(
        functools.partial(_ffn_kernel, tiles_per_seq=s // tm, final_norm=final_norm),
        grid=(b * s // tm,),
        in_specs=[tok] + [const(w) for w in weights],
        out_specs=tok,
        out_shape=jax.ShapeDtypeStruct((b * s, d), F32),
        scratch_shapes=[pltpu.VMEM((tm, f), BF16), pltpu.VMEM((SUBLANES, f), F32)],
        compiler_params=_cparams(("arbitrary",)),
        name="ffn",
    )(xf, *weights)
    return out.reshape(b, s, d)


def _sgu_kernel(x_ref, vec_ref, win_ref, ws_ref, bst_ref, expand_ref, wout_ref, out_ref,
                wc_sc, bias_sc, *, n_sub):
    d = x_ref.shape[-1]
    tm = x_ref.shape[0]
    C, G = SGU_CHUNK, SGU_GROUPS
    gd = d // G
    gpb = MXU_TILE // gd

    nblk = d // MXU_TILE

    @pl.when((pl.program_id(0) == 0) & (pl.program_id(1) == 0))
    def _():
        causal = _iota((C, C), 0) >= _iota((C, C), 1)
        for g in range(G):
            wc_sc[g // gpb, :, (g % gpb) * C:(g % gpb + 1) * C] = (
                jnp.where(causal, ws_ref[g], 0.0).astype(BF16))
        bias_sc[...] = _dot_exact_rhs(bst_ref[...], expand_ref[...])

    vec = vec_ref[...]
    gn, ln_g, ln_b = vec[0:1, :], vec[1:2, :], vec[2:3, :]
    lane_grp = _iota((C, MXU_TILE), 1) // gd
    bias = bias_sc[...]
    ts = tm // n_sub
    subs = range(n_sub)
    rsl = [slice(s * ts, (s + 1) * ts) for s in subs]
    x = [x_ref[rsl[s], :] for s in subs]
    z = [jax.nn.gelu(_dot(_rms(x[s], gn), win_ref[...])) for s in subs]
    vn = []
    for s in subs:
        v = z[s][:, d:]
        dv = v - jnp.mean(v, axis=-1, keepdims=True)
        var = jnp.mean(dv * dv, axis=-1, keepdims=True)
        vn.append((dv * lax.rsqrt(var + SGU_LN_EPS) * ln_g + ln_b).astype(BF16))
    gated = []
    for s in subs:
        chunks = []
        for c in range(ts // C):
            blocks = []
            for blk in range(nblk):
                vb = vn[s][c * C:(c + 1) * C, blk * MXU_TILE:(blk + 1) * MXU_TILE]
                rhs = jnp.concatenate(
                    [jnp.where(lane_grp == gi, vb, jnp.zeros_like(vb)) for gi in range(gpb)],
                    axis=0)
                blocks.append(jnp.dot(wc_sc[blk], rhs, preferred_element_type=F32))
            chunks.append(jnp.concatenate(blocks, axis=-1) + bias)
        gated.append((z[s][:, :d] * jnp.concatenate(chunks, axis=0)).astype(BF16))
    for s in subs:
        out_ref[rsl[s], :] = x[s] + jnp.dot(gated[s], wout_ref[...], preferred_element_type=F32)


def _sgu(x, vecs, w_in, w_s, b_s, w_out, tm, n_sub):
    b, s, d = x.shape
    tm = min(tm, s)
    gd = d // SGU_GROUPS
    gpb = MXU_TILE // gd
    expand = (jnp.arange(d)[None, :] // gd == jnp.arange(SGU_CHUNK)[:, None]).astype(BF16)
    bst = jnp.pad(b_s.T, ((0, 0), (0, SGU_CHUNK - SGU_GROUPS)))
    tok = pl.BlockSpec((None, tm, d), lambda bi, i: (bi, i, 0))
    full = lambda arr: pl.BlockSpec(arr.shape, lambda bi, i: (0,) * arr.ndim)
    ins = (vecs, w_in, w_s, bst, expand, w_out)
    return pl.pallas_call(
        functools.partial(_sgu_kernel, n_sub=n_sub),
        grid=(b, s // tm),
        in_specs=[tok] + [full(a) for a in ins],
        out_specs=tok,
        out_shape=jax.ShapeDtypeStruct((b, s, d), F32),
        scratch_shapes=[pltpu.VMEM((SGU_GROUPS // gpb, SGU_CHUNK, gpb * SGU_CHUNK), BF16),
                        pltpu.VMEM((SGU_CHUNK, d), F32)],
        compiler_params=_cparams(("arbitrary", "arbitrary")),
        name="sgu",
    )(x, *ins)


def _pad_rows(rows, d):
    n = -(-len(rows) // SUBLANES) * SUBLANES
    arr = jnp.stack([r.reshape(d).astype(F32) for r in rows])
    return jnp.pad(arr, ((0, n - len(rows)), (0, 0)))


def kernel(x, mem, norm_mix, norm_mem, norm_ffn, norm_final, mem_norm, rw_mu, rw_w_rkv, rw_w0, rw_w1, rw_w2, rw_a0, rw_a1, rw_a2, rw_g1, rw_g2, rw_k_k, rw_k_a, rw_r_k, rw_lnx_g, rw_lnx_b, rw_w_o, gm_w_in, gm_ln_g, gm_ln_b, gm_w_s, gm_b_s, gm_w_out, ca_w_q, ca_w_kv, ca_w_o, ff_w_up, ff_conv_w, ff_conv_b, ff_w_down):
    d = x.shape[-1]
    depth = norm_mix.shape[0]
    bf = lambda w: w.astype(BF16)
    lane = jnp.arange(MXU_TILE)
    ones_bd = (lane[:, None] // RWKV_HEAD_DIM == lane[None, :] // RWKV_HEAD_DIM).astype(BF16)
    kv = _mem_kv(mem, mem_norm, ca_w_kv)
    for i in range(depth):
        j = i // 2
        if i % 2 == 0:
            vecs = _pad_rows([norm_mix[i]] + [rw_mu[j, n] for n in range(6)]
                             + [rw_w0[j], rw_a0[j], rw_k_k[j], rw_k_a[j], rw_r_k[j]], d)
            r, lw, k, v, kk, b, g, bonus = _rwkv_proj(
                x, vecs, ones_bd, bf(rw_w_rkv[j]), bf(rw_w1[j]), bf(rw_w2[j]), bf(rw_a1[j]),
                bf(rw_a2[j]), bf(rw_g1[j]), bf(rw_g2[j]), tm=512, n_sub=2)
            o = _wkv(r, lw, k, v, kk, b, tc=256, lanes=1024)
            x = _rwkv_out_xattn(x, o, g, bonus, _pad_rows([rw_lnx_g[j], rw_lnx_b[j]], d),
                                ones_bd, bf(rw_w_o[j]), norm_mem[i].reshape(1, d),
                                bf(ca_w_q[i]), kv[i], bf(ca_w_o[i]), tm=1024)
        else:
            x = _sgu(x, _pad_rows([norm_mix[i], gm_ln_g[j], gm_ln_b[j]], d), bf(gm_w_in[j]),
                     gm_w_s[j], gm_b_s[j], bf(gm_w_out[j]), tm=1024, n_sub=2)
            x = _xattn(x, norm_mem[i].reshape(1, d), bf(ca_w_q[i]), kv[i], bf(ca_w_o[i]),
                       tm=1024)
        x = _ffn(x, norm_ffn[i].reshape(1, d), bf(ff_w_up[i]), ff_conv_w[i],
                 ff_conv_b[i].reshape(1, -1), bf(ff_w_down[i]), norm_final.reshape(1, d),
                 final_norm=(i == depth - 1), tm=1024)
    return x
```

```python
import functools
import math

import jax
import jax.numpy as jnp
from jax import lax
from jax.experimental import pallas as pl
from jax.experimental.pallas import tpu as pltpu

F32 = jnp.float32
BF16 = jnp.bfloat16

RWKV_HEAD_DIM = 64
RWKV_GN_EPS = 64e-5
SGU_CHUNK = 128
SGU_GROUPS = 16
SGU_LN_EPS = 1e-5
XATTN_HEADS = 4
RMS_EPS = 1e-6

MXU_TILE = 256
SUBLANES = 8
WKV_CHUNK = 64
WKV_PAIR = 2 * RWKV_HEAD_DIM
VMEM_LIMIT = 56 * 1024 * 1024


def _cparams(sem):
    return pltpu.CompilerParams(dimension_semantics=sem, vmem_limit_bytes=VMEM_LIMIT)


def _dot(a, b):
    return jnp.dot(a.astype(BF16), b.astype(BF16), preferred_element_type=F32)


def _dot_nt(a, b):
    return lax.dot_general(a.astype(BF16), b.astype(BF16), (((1,), (1,)), ((), ())),
                           preferred_element_type=F32)


def _dot_tn(a, b):
    return lax.dot_general(a.astype(BF16), b.astype(BF16), (((0,), (0,)), ((), ())),
                           preferred_element_type=F32)


def _split3(x):
    hi = x.astype(BF16)
    r1 = x - hi.astype(F32)
    mid = r1.astype(BF16)
    lo = (r1 - mid.astype(F32)).astype(BF16)
    return hi, mid, lo


def _dot_exact_lhs(a_bf16, x):
    return sum(jnp.dot(a_bf16, t, preferred_element_type=F32) for t in _split3(x))


def _dot_exact_rhs(x, b_bf16):
    return sum(jnp.dot(t, b_bf16, preferred_element_type=F32) for t in _split3(x))


def _rms(x, g):
    return x * lax.rsqrt(jnp.mean(x * x, axis=-1, keepdims=True) + RMS_EPS) * g


def _sigmoid(x):
    return 1.0 / (1.0 + jnp.exp(-x))


def _group_sum(x, ones_bd):
    d = x.shape[-1]
    parts = [_dot(x[:, j:j + MXU_TILE], ones_bd) for j in range(0, d, MXU_TILE)]
    return jnp.concatenate(parts, axis=-1)


def _iota(shape, dim):
    return lax.broadcasted_iota(jnp.int32, shape, dim)


def _mem_kv_kernel(mem_ref, g_ref, w_ref, o_ref):
    mn = _rms(mem_ref[...], g_ref[...])
    o_ref[...] = _dot(mn, w_ref[...]).astype(o_ref.dtype)


def _mem_kv(mem, mem_norm, w_kv):
    b, m, d = mem.shape
    nl = w_kv.shape[0]
    return pl.pallas_call(
        _mem_kv_kernel,
        grid=(nl, b),
        in_specs=[pl.BlockSpec((None, m, d), lambda l, i: (i, 0, 0)),
                  pl.BlockSpec((1, d), lambda l, i: (0, 0)),
                  pl.BlockSpec((None, d, 2 * d), lambda l, i: (l, 0, 0))],
        out_specs=pl.BlockSpec((None, None, m, 2 * d), lambda l, i: (l, i, 0, 0)),
        out_shape=jax.ShapeDtypeStruct((nl, b, m, 2 * d), BF16),
        compiler_params=_cparams(("arbitrary", "arbitrary")),
        name="mem_kv",
    )(mem, mem_norm.reshape(1, d), w_kv.astype(BF16))


def _rwkv_proj_kernel(x_ref, xp_ref, vec_ref, ones_ref, wrkv_ref, w1_ref, w2_ref, a1_ref, a2_ref,
                      g1_ref, g2_ref,
                      r_ref, lw_ref, k_ref, v_ref, kk_ref, b_ref, g_ref, bonus_ref, *, n_sub):
    i = pl.program_id(1)
    vec = vec_ref[...]
    row = lambda n: vec[n:n + 1, :]
    gn, w0, a0, k_k, k_a, r_k = row(0), row(7), row(8), row(9), row(10), row(11)
    ones_bd = ones_ref[...]
    ts = x_ref.shape[0] // n_sub
    subs = range(n_sub)
    rsl = [slice(u * ts, (u + 1) * ts) for u in subs]
    h = [_rms(x_ref[rsl[u], :], gn) for u in subs]
    hp = _rms(xp_ref[...], gn)[SUBLANES - 1:SUBLANES, :]
    hp = jnp.where(i == 0, 0.0, hp)
    first = _iota((ts, h[0].shape[1]), 0) == 0
    lasts = [hp] + [h[u][ts - 1:ts, :] for u in subs[:-1]]
    dx = [jnp.where(first, lasts[u], pltpu.roll(h[u], 1, axis=0)) - h[u] for u in subs]
    hb = [h[u].astype(BF16) for u in subs]
    dxb = [dx[u].astype(BF16) for u in subs]
    mix = lambda u, j: hb[u] + dxb[u] * row(1 + j).astype(BF16)

    def post_steps(u, raw):
        st = {}

        def s0():
            lw_ref[rsl[u], :] = -math.exp(-0.5) * _sigmoid(w0 + raw["wl"])
            st["a"] = _sigmoid(a0 + raw["al"])
            g_ref[rsl[u], :] = raw["g"].astype(g_ref.dtype)

        def s1():
            kk_raw = raw["k"] * k_k
            kk = kk_raw * lax.rsqrt(jnp.maximum(_group_sum(kk_raw * kk_raw, ones_bd), 1e-24))
            kk_ref[rsl[u], :] = kk.astype(kk_ref.dtype)
            b_ref[rsl[u], :] = (kk * st["a"]).astype(b_ref.dtype)

        def s2():
            k2 = raw["k"] * (1.0 + (st["a"] - 1.0) * k_a)
            bonus = _group_sum(raw["r"] * k2 * r_k, ones_bd) * raw["v"]
            k_ref[rsl[u], :] = k2.astype(k_ref.dtype)
            bonus_ref[rsl[u], :] = bonus.astype(bonus_ref.dtype)
            r_ref[rsl[u], :] = raw["r"].astype(r_ref.dtype)
            v_ref[rsl[u], :] = raw["v"].astype(v_ref.dtype)

        return [s0, s1, s2]

    pending = []
    for u in subs:
        step = lambda: pending.pop(0)() if pending else None
        raw = {}
        raw["wl"] = _dot(jnp.tanh(_dot(mix(u, 1), w1_ref[...])), w2_ref[...])
        raw["al"] = _dot(_dot(mix(u, 4), a1_ref[...]), a2_ref[...])
        raw["g"] = _dot(_sigmoid(_dot(mix(u, 5), g1_ref[...])), g2_ref[...])
        raw["k"] = _dot(mix(u, 2), wrkv_ref[1])
        step()
        raw["r"] = _dot(mix(u, 0), wrkv_ref[0])
        step()
        raw["v"] = _dot(mix(u, 3), wrkv_ref[2])
        step()
        pending = post_steps(u, raw)
    for fn in pending:
        fn()


def _rwkv_proj(x, vecs, ones_bd, w_rkv, w1, w2, a1, a2, g1, g2, tm, n_sub):
    b, s, d = x.shape
    tm = min(tm, s)
    tok = pl.BlockSpec((None, tm, d), lambda bi, i: (bi, i, 0))
    full = lambda arr: pl.BlockSpec(arr.shape, lambda bi, i: (0,) * arr.ndim,
                                    pipeline_mode=pl.Buffered(1))
    prev = pl.BlockSpec((None, SUBLANES, d),
                        lambda bi, i: (bi, jnp.maximum(i * (tm // SUBLANES) - 1, 0), 0))
    weights = (vecs, ones_bd, w_rkv, w1, w2, a1, a2, g1, g2)
    out_dt = (BF16, F32, BF16, BF16, BF16, BF16, BF16, BF16)
    return pl.pallas_call(
        functools.partial(_rwkv_proj_kernel, n_sub=n_sub),
        grid=(b, s // tm),
        in_specs=[tok, prev] + [full(w) for w in weights],
        out_specs=[tok] * 8,
        out_shape=[jax.ShapeDtypeStruct((b, s, d), dt) for dt in out_dt],
        compiler_params=_cparams(("arbitrary", "arbitrary")),
        name="rwkv_proj",
    )(x, x, *weights)


def _wkv_kernel(r_ref, lw_ref, k_ref, v_ref, kk_ref, b_ref, tri_ref, o_ref, st_ref,
                *, n_chunks, n_pairs):
    L, W = WKV_CHUNK, WKV_PAIR
    nh = W // RWKV_HEAD_DIM

    @pl.when(pl.program_id(2) == 0)
    def _():
        st_ref[...] = jnp.zeros_like(st_ref)

    ri = _iota((W, W), 0)
    ci = _iota((W, W), 1)
    strict = ri > ci
    diag = ri == ci
    incl2 = _iota((W, 2 * W), 0) >= (_iota((W, 2 * W), 1) % W)
    lane_head = _iota((L, W), 1) // RWKV_HEAD_DIM
    tri = tri_ref[...]

    def stack(x):
        return jnp.concatenate([jnp.where(lane_head == h, x, 0.0) for h in range(nh)],
                               axis=0).astype(BF16)

    units = [(c, p) for c in range(n_chunks) for p in range(n_pairs)]
    nu = len(units)
    at, rt, vs, bk, bkh, wlast = [], [], [], [], [], []
    for c in range(n_chunks):
        rows = slice(c * L, (c + 1) * L)
        lw = lw_ref[rows, :]
        cum = _dot_exact_lhs(tri, lw)
        cum_last = cum[L - 1:L, :]
        w_inv = jnp.exp(-cum)
        w_tail = jnp.exp(cum_last - cum)
        w_last = jnp.exp(cum_last)
        k = k_ref[rows, :].astype(F32)
        b = b_ref[rows, :].astype(F32)
        rt_c = r_ref[rows, :].astype(F32) * jnp.exp(cum)
        at_c = -kk_ref[rows, :].astype(F32) * jnp.exp(cum - lw)
        v_c = v_ref[rows, :].astype(F32)
        bt_c, kt_c, bh_c, kh_c = b * w_inv, k * w_inv, b * w_tail, k * w_tail
        for p in range(n_pairs):
            ln = slice(p * W, (p + 1) * W)
            rt.append(stack(rt_c[:, ln]))
            at.append(stack(at_c[:, ln]))
            vs.append(stack(v_c[:, ln]))
            bk.append(jnp.concatenate([stack(bt_c[:, ln]), stack(kt_c[:, ln])], axis=0))
            bkh.append(jnp.concatenate([stack(bh_c[:, ln]), stack(kh_c[:, ln])], axis=0))
            wlast.append(w_last[:, ln])
    a_all = [_dot_nt(at[u], bk[u]) for u in range(nu)]
    r_all = [_dot_nt(rt[u], bk[u]) for u in range(nu)]
    nil = [jnp.where(strict, a_all[u][:, :W], 0.0) for u in range(nu)]
    a_ak = [jnp.where(strict, a_all[u][:, W:], 0.0).astype(BF16) for u in range(nu)]
    a_r = [jnp.where(incl2, r_all[u], 0.0).astype(BF16) for u in range(nu)]
    av = [_dot(a_ak[u], vs[u]).astype(BF16) for u in range(nu)]
    t = [jnp.where(diag, 1.0, nil[u]) for u in range(nu)]
    pw = [nil[u].astype(BF16) for u in range(nu)]
    pw = [_dot(pw[u], pw[u]).astype(BF16) for u in range(nu)]
    for _ in range(4):
        both = [_dot(pw[u], jnp.concatenate([pw[u], t[u].astype(BF16)], axis=1))
                for u in range(nu)]
        pw = [both[u][:, :W].astype(BF16) for u in range(nu)]
        t = [t[u] + both[u][:, W:] for u in range(nu)]
    t = [t[u] + _dot(pw[u], t[u]) for u in range(nu)]
    pu = [_dot(t[u], jnp.concatenate([at[u], av[u]], axis=1)).astype(BF16)
          for u in range(nu)]
    zero = jnp.zeros((W, W), BF16)
    rhs = [jnp.concatenate([pu[u], jnp.concatenate([zero, vs[u]], axis=1)], axis=0)
           for u in range(nu)]
    qo = [_dot(a_r[u], rhs[u]) for u in range(nu)]
    gh = [_dot_tn(bkh[u], rhs[u]) for u in range(nu)]
    qg_lhs = [jnp.concatenate(
        [(rt[u].astype(F32) + qo[u][:, :W]).astype(BF16),
         (jnp.where(diag, wlast[u], 0.0) + gh[u][:, :W]).astype(BF16)], axis=0)
        for u in range(nu)]
    states = [st_ref[p] for p in range(n_pairs)]
    for u, (c, p) in enumerate(units):
        qg = _dot(qg_lhs[u], states[p])
        o_s = qg[:W] + qo[u][:, W:]
        o_ref[c * L:(c + 1) * L, p * W:(p + 1) * W] = (o_s[:L] + o_s[L:]).astype(o_ref.dtype)
        states[p] = qg[W:] + gh[u][:, W:]
    for p in range(n_pairs):
        st_ref[p] = states[p]


def _wkv(r, lw, k, v, kk, b, tc, lanes):
    bsz, s, d = r.shape
    tc = min(tc, s)
    n_chunks = tc // WKV_CHUNK
    W = min(lanes, d)
    n_pairs = W // WKV_PAIR
    tri = (jnp.arange(WKV_CHUNK)[:, None] >= jnp.arange(WKV_CHUNK)[None, :]).astype(BF16)
    blk = pl.BlockSpec((None, tc, W), lambda bi, q, i: (bi, i, q))
    return pl.pallas_call(
        functools.partial(_wkv_kernel, n_chunks=n_chunks, n_pairs=n_pairs),
        grid=(bsz, d // W, s // tc),
        in_specs=[blk] * 6 + [pl.BlockSpec(tri.shape, lambda bi, q, i: (0, 0))],
        out_specs=blk,
        out_shape=jax.ShapeDtypeStruct((bsz, s, d), BF16),
        scratch_shapes=[pltpu.VMEM((n_pairs, WKV_PAIR, WKV_PAIR), F32)],
        compiler_params=_cparams(("arbitrary", "arbitrary", "arbitrary")),
        name="wkv",
    )(r, lw, k, v, kk, b, tri)


def _rwkv_out_math(x, o, g, bonus, vec, ones_bd, w_o):
    lnx_g, lnx_b = vec[0:1, :], vec[1:2, :]
    o = o.astype(F32)
    inv_n = 1.0 / RWKV_HEAD_DIM
    mean = _group_sum(o, ones_bd) * inv_n
    dlt = o - mean
    var = _group_sum(dlt * dlt, ones_bd) * inv_n
    y = dlt * lax.rsqrt(var + RWKV_GN_EPS) * lnx_g + lnx_b + bonus.astype(F32)
    return x + _dot(y * g.astype(F32), w_o)


def _xattn_math(xs, g, wq_ref, kv_ref, wo_ref):
    d = xs[0].shape[-1]
    hd = d // XATTN_HEADS
    units = [(u, h) for u in range(len(xs)) for h in range(XATTN_HEADS)]
    q = [_dot(_rms(x, g), wq_ref[...]) for x in xs]
    s = [_dot_nt(q[u][:, h * hd:(h + 1) * hd], kv_ref[:, h * hd:(h + 1) * hd]) * (hd ** -0.5)
         for u, h in units]
    p = [jnp.exp(si - jnp.max(si, axis=-1, keepdims=True)) for si in s]
    p = [pi * (1.0 / jnp.sum(pi, axis=-1, keepdims=True)) for pi in p]
    o = [_dot(p[i], kv_ref[:, d + h * hd:d + (h + 1) * hd]).astype(BF16)
         for i, (u, h) in enumerate(units)]
    return [x + _dot(jnp.concatenate(o[u * XATTN_HEADS:(u + 1) * XATTN_HEADS], axis=-1),
                     wo_ref[...]) for u, x in enumerate(xs)]


def _row_tiles(ref, n_sub):
    ts = ref.shape[0] // n_sub
    return [slice(u * ts, (u + 1) * ts) for u in range(n_sub)]


def _xattn_kernel(x_ref, g_ref, wq_ref, kv_ref, wo_ref, out_ref, *, n_sub):
    rsl = _row_tiles(x_ref, n_sub)
    ys = _xattn_math([x_ref[r, :] for r in rsl], g_ref[...], wq_ref, kv_ref, wo_ref)
    for r, y in zip(rsl, ys):
        out_ref[r, :] = y


def _rwkv_out_xattn_kernel(x_ref, o_ref, g_ref, bonus_ref, vec_ref, ones_ref, wo1_ref,
                           gx_ref, wq_ref, kv_ref, wo_ref, out_ref, *, n_sub):
    rsl = _row_tiles(x_ref, n_sub)
    xs = [_rwkv_out_math(x_ref[r, :], o_ref[r, :], g_ref[r, :], bonus_ref[r, :], vec_ref[...],
                         ones_ref[...], wo1_ref[...]) for r in rsl]
    ys = _xattn_math(xs, gx_ref[...], wq_ref, kv_ref, wo_ref)
    for r, y in zip(rsl, ys):
        out_ref[r, :] = y


def _rwkv_out_xattn(x, o, g, bonus, vecs, ones_bd, w_o1, gx, w_q, kv, w_o, tm, n_sub):
    b, s, d = x.shape
    tm = min(tm, s)
    m = kv.shape[1]
    tok = pl.BlockSpec((None, tm, d), lambda bi, i: (bi, i, 0))
    full = lambda arr: pl.BlockSpec(arr.shape, lambda bi, i: (0,) * arr.ndim,
                                    pipeline_mode=pl.Buffered(1))
    return pl.pallas_call(
        functools.partial(_rwkv_out_xattn_kernel, n_sub=n_sub),
        grid=(b, s // tm),
        in_specs=[tok, tok, tok, tok, full(vecs), full(ones_bd), full(w_o1), full(gx), full(w_q),
                  pl.BlockSpec((None, m, 2 * d), lambda bi, i: (bi, 0, 0)), full(w_o)],
        out_specs=tok,
        out_shape=jax.ShapeDtypeStruct((b, s, d), F32),
        compiler_params=_cparams(("arbitrary", "arbitrary")),
        name="rwkv_out_xattn",
    )(x, o, g, bonus, vecs, ones_bd, w_o1, gx, w_q, kv, w_o)


def _xattn(x, g, w_q, kv, w_o, tm, n_sub):
    b, s, d = x.shape
    tm = min(tm, s)
    m = kv.shape[1]
    tok = pl.BlockSpec((None, tm, d), lambda bi, i: (bi, i, 0))
    full = lambda arr: pl.BlockSpec(arr.shape, lambda bi, i: (0,) * arr.ndim)
    return pl.pallas_call(
        functools.partial(_xattn_kernel, n_sub=n_sub),
        grid=(b, s // tm),
        in_specs=[tok, full(g), full(w_q),
                  pl.BlockSpec((None, m, 2 * d), lambda bi, i: (bi, 0, 0)), full(w_o)],
        out_specs=tok,
        out_shape=jax.ShapeDtypeStruct((b, s, d), F32),
        compiler_params=_cparams(("arbitrary", "arbitrary")),
        name="xattn",
    )(x, g, w_q, kv, w_o)


def _ffn_kernel(x_ref, g_ref, wup_ref, cw_ref, cb_ref, wd_ref, gf_ref, out_ref,
                act_sc, carry_sc, *, tiles_per_seq, final_norm, n_sub):
    f = wd_ref.shape[0]
    fc = MXU_TILE
    rsl = _row_tiles(x_ref, n_sub)
    ts = x_ref.shape[0] // n_sub

    @pl.when((pl.program_id(0) % tiles_per_seq) == 0)
    def _():
        carry_sc[...] = jnp.zeros_like(carry_sc)

    xn = [_rms(x_ref[r, :], g_ref[...]).astype(BF16) for r in rsl]
    rows = _iota((ts, fc), 0)
    for j in range(f // fc):
        cols = slice(j * fc, (j + 1) * fc)
        prev = carry_sc[:, cols]
        for u, r in enumerate(rsl):
            gate = jnp.dot(xn[u], wup_ref[:, cols], preferred_element_type=F32)
            val = jnp.dot(xn[u], wup_ref[:, f + j * fc:f + (j + 1) * fc],
                          preferred_element_type=F32)
            s1 = jnp.where(rows == 0, prev[SUBLANES - 1:SUBLANES, :],
                           pltpu.roll(gate, 1, axis=0))
            s2 = pltpu.roll(gate, 2, axis=0)
            s2 = jnp.where(rows == 0, prev[SUBLANES - 2:SUBLANES - 1, :], s2)
            s2 = jnp.where(rows == 1, prev[SUBLANES - 1:SUBLANES, :], s2)
            conv = (s2 * cw_ref[0:1, cols] + s1 * cw_ref[1:2, cols] + gate * cw_ref[2:3, cols]
                    + cb_ref[:, cols])
            act_sc[r, cols] = (conv * _sigmoid(conv) * val).astype(BF16)
            prev = gate[ts - SUBLANES:ts, :]
        carry_sc[:, cols] = prev
    for r in rsl:
        y = x_ref[r, :] + jnp.dot(act_sc[r, :], wd_ref[...], preferred_element_type=F32)
        out_ref[r, :] = _rms(y, gf_ref[...]) if final_norm else y


def _ffn(x, g, w_up, conv_w, conv_b, w_down, g_final, final_norm, tm, n_sub):
    b, s, d = x.shape
    f = w_down.shape[0]
    tm = min(tm, s)
    xf = x.reshape(b * s, d)
    tok = pl.BlockSpec((tm, d), lambda t: (t, 0))
    const = lambda arr: pl.BlockSpec(arr.shape, lambda t: (0,) * arr.ndim,
                                     pipeline_mode=pl.Buffered(1))
    weights = (g, w_up, conv_w, conv_b, w_down, g_final)
    out = pl.pallas_call(
        functools.partial(_ffn_kernel, tiles_per_seq=s // tm, final_norm=final_norm,
                          n_sub=n_sub),
        grid=(b * s // tm,),
        in_specs=[tok] + [const(w) for w in weights],
        out_specs=tok,
        out_shape=jax.ShapeDtypeStruct((b * s, d), F32),
        scratch_shapes=[pltpu.VMEM((tm, f), BF16), pltpu.VMEM((SUBLANES, f), F32)],
        compiler_params=_cparams(("arbitrary",)),
        name="ffn",
    )(xf, *weights)
    return out.reshape(b, s, d)


def _sgu_kernel(x_ref, vec_ref, win_ref, ws_ref, bst_ref, expand_ref, wout_ref, out_ref,
                wc_sc, bias_sc, *, n_sub):
    d = x_ref.shape[-1]
    tm = x_ref.shape[0]
    C, G = SGU_CHUNK, SGU_GROUPS
    gd = d // G
    gpb = MXU_TILE // gd

    nblk = d // MXU_TILE

    @pl.when((pl.program_id(0) == 0) & (pl.program_id(1) == 0))
    def _():
        causal = _iota((C, C), 0) >= _iota((C, C), 1)
        for g in range(G):
            wc_sc[g // gpb, :, (g % gpb) * C:(g % gpb + 1) * C] = (
                jnp.where(causal, ws_ref[g], 0.0).astype(BF16))
        bias_sc[...] = _dot_exact_rhs(bst_ref[...], expand_ref[...])

    vec = vec_ref[...]
    gn, ln_g, ln_b = vec[0:1, :], vec[1:2, :], vec[2:3, :]
    lane_grp = _iota((C, MXU_TILE), 1) // gd
    bias = bias_sc[...]
    ts = tm // n_sub
    subs = range(n_sub)
    rsl = [slice(s * ts, (s + 1) * ts) for s in subs]
    x = [x_ref[rsl[s], :] for s in subs]
    z = [jax.nn.gelu(_dot(_rms(x[s], gn), win_ref[...])) for s in subs]
    vn = []
    for s in subs:
        v = z[s][:, d:]
        dv = v - jnp.mean(v, axis=-1, keepdims=True)
        var = jnp.mean(dv * dv, axis=-1, keepdims=True)
        vn.append((dv * lax.rsqrt(var + SGU_LN_EPS) * ln_g + ln_b).astype(BF16))
    gated = []
    for s in subs:
        chunks = []
        for c in range(ts // C):
            blocks = []
            for blk in range(nblk):
                vb = vn[s][c * C:(c + 1) * C, blk * MXU_TILE:(blk + 1) * MXU_TILE]
                rhs = jnp.concatenate(
                    [jnp.where(lane_grp == gi, vb, jnp.zeros_like(vb)) for gi in range(gpb)],
                    axis=0)
                blocks.append(jnp.dot(wc_sc[blk], rhs, preferred_element_type=F32))
            chunks.append(jnp.concatenate(blocks, axis=-1) + bias)
        gated.append((z[s][:, :d] * jnp.concatenate(chunks, axis=0)).astype(BF16))
    for s in subs:
        out_ref[rsl[s], :] = x[s] + jnp.dot(gated[s], wout_ref[...], preferred_element_type=F32)


def _sgu(x, vecs, w_in, w_s, b_s, w_out, tm, n_sub):
    b, s, d = x.shape
    tm = min(tm, s)
    gd = d // SGU_GROUPS
    gpb = MXU_TILE // gd
    expand = (jnp.arange(d)[None, :] // gd == jnp.arange(SGU_CHUNK)[:, None]).astype(BF16)
    bst = jnp.pad(b_s.T, ((0, 0), (0, SGU_CHUNK - SGU_GROUPS)))
    tok = pl.BlockSpec((None, tm, d), lambda bi, i: (bi, i, 0))
    full = lambda arr: pl.BlockSpec(arr.shape, lambda bi, i: (0,) * arr.ndim)
    ins = (vecs, w_in, w_s, bst, expand, w_out)
    return pl.pallas_call(
        functools.partial(_sgu_kernel, n_sub=n_sub),
        grid=(b, s // tm),
        in_specs=[tok] + [full(a) for a in ins],
        out_specs=tok,
        out_shape=jax.ShapeDtypeStruct((b, s, d), F32),
        scratch_shapes=[pltpu.VMEM((SGU_GROUPS // gpb, SGU_CHUNK, gpb * SGU_CHUNK), BF16),
                        pltpu.VMEM((SGU_CHUNK, d), F32)],
        compiler_params=_cparams(("arbitrary", "arbitrary")),
        name="sgu",
    )(x, *ins)


def _pad_rows(rows, d):
    n = -(-len(rows) // SUBLANES) * SUBLANES
    arr = jnp.stack([r.reshape(d).astype(F32) for r in rows])
    return jnp.pad(arr, ((0, n - len(rows)), (0, 0)))


def kernel(x, mem, norm_mix, norm_mem, norm_ffn, norm_final, mem_norm, rw_mu, rw_w_rkv, rw_w0, rw_w1, rw_w2, rw_a0, rw_a1, rw_a2, rw_g1, rw_g2, rw_k_k, rw_k_a, rw_r_k, rw_lnx_g, rw_lnx_b, rw_w_o, gm_w_in, gm_ln_g, gm_ln_b, gm_w_s, gm_b_s, gm_w_out, ca_w_q, ca_w_kv, ca_w_o, ff_w_up, ff_conv_w, ff_conv_b, ff_w_down):
    d = x.shape[-1]
    depth = norm_mix.shape[0]
    bf = lambda w: w.astype(BF16)
    lane = jnp.arange(MXU_TILE)
    ones_bd = (lane[:, None] // RWKV_HEAD_DIM == lane[None, :] // RWKV_HEAD_DIM).astype(BF16)
    kv = _mem_kv(mem, mem_norm, ca_w_kv)
    for i in range(depth):
        j = i // 2
        if i % 2 == 0:
            vecs = _pad_rows([norm_mix[i]] + [rw_mu[j, n] for n in range(6)]
                             + [rw_w0[j], rw_a0[j], rw_k_k[j], rw_k_a[j], rw_r_k[j]], d)
            r, lw, k, v, kk, b, g, bonus = _rwkv_proj(
                x, vecs, ones_bd, bf(rw_w_rkv[j]), bf(rw_w1[j]), bf(rw_w2[j]), bf(rw_a1[j]),
                bf(rw_a2[j]), bf(rw_g1[j]), bf(rw_g2[j]), tm=512, n_sub=2)
            o = _wkv(r, lw, k, v, kk, b, tc=256, lanes=1024)
            x = _rwkv_out_xattn(x, o, g, bonus, _pad_rows([rw_lnx_g[j], rw_lnx_b[j]], d),
                                ones_bd, bf(rw_w_o[j]), norm_mem[i].reshape(1, d),
                                bf(ca_w_q[i]), kv[i], bf(ca_w_o[i]), tm=1024, n_sub=1)
        else:
            x = _sgu(x, _pad_rows([norm_mix[i], gm_ln_g[j], gm_ln_b[j]], d), bf(gm_w_in[j]),
                     gm_w_s[j], gm_b_s[j], bf(gm_w_out[j]), tm=1024, n_sub=2)
            x = _xattn(x, norm_mem[i].reshape(1, d), bf(ca_w_q[i]), kv[i], bf(ca_w_o[i]),
                       tm=1024, n_sub=1)
        x = _ffn(x, norm_ffn[i].reshape(1, d), bf(ff_w_up[i]), ff_conv_w[i],
                 ff_conv_b[i].reshape(1, -1), bf(ff_w_down[i]), norm_final.reshape(1, d),
                 final_norm=(i == depth - 1), tm=1024, n_sub=2)
    return x
```

```python
import functools
import math

import jax
import jax.numpy as jnp
from jax import lax
from jax.experimental import pallas as pl
from jax.experimental.pallas import tpu as pltpu

F32 = jnp.float32
BF16 = jnp.bfloat16

RWKV_HEAD_DIM = 64
RWKV_GN_EPS = 64e-5
SGU_CHUNK = 128
SGU_GROUPS = 16
SGU_LN_EPS = 1e-5
XATTN_HEADS = 4
RMS_EPS = 1e-6

MXU_TILE = 256
SUBLANES = 8
VMEM_BYTES = 64 * 1024 * 1024
VMEM_LIMIT = VMEM_BYTES - VMEM_BYTES // 8
WKV_CHUNK = 64
WKV_PAIR = 2 * RWKV_HEAD_DIM

TILES = {
    "rwkv_proj": (512, 2),
    "wkv": (256, 1),
    "rwkv_out_xattn": (1024, 1),
    "xattn": (1024, 1),
    "sgu": (1024, 2),
    "ffn": (1024, 2),
}


def _cparams(sem):
    return pltpu.CompilerParams(dimension_semantics=sem, vmem_limit_bytes=VMEM_LIMIT)


def _dot(a, b):
    return jnp.dot(a.astype(BF16), b.astype(BF16), preferred_element_type=F32)


def _dot_nt(a, b):
    return lax.dot_general(a.astype(BF16), b.astype(BF16), (((1,), (1,)), ((), ())),
                           preferred_element_type=F32)


def _dot_tn(a, b):
    return lax.dot_general(a.astype(BF16), b.astype(BF16), (((0,), (0,)), ((), ())),
                           preferred_element_type=F32)


def _split3(x):
    hi = x.astype(BF16)
    r1 = x - hi.astype(F32)
    mid = r1.astype(BF16)
    lo = (r1 - mid.astype(F32)).astype(BF16)
    return hi, mid, lo


def _dot_exact_lhs(a_bf16, x):
    return sum(jnp.dot(a_bf16, t, preferred_element_type=F32) for t in _split3(x))


def _dot_exact_rhs(x, b_bf16):
    return sum(jnp.dot(t, b_bf16, preferred_element_type=F32) for t in _split3(x))


def _rms(x, g):
    return x * lax.rsqrt(jnp.mean(x * x, axis=-1, keepdims=True) + RMS_EPS) * g


def _sigmoid(x):
    return 1.0 / (1.0 + jnp.exp(-x))


def _group_sum(x, ones_bd):
    d = x.shape[-1]
    parts = [_dot(x[:, j:j + MXU_TILE], ones_bd) for j in range(0, d, MXU_TILE)]
    return jnp.concatenate(parts, axis=-1)


def _iota(shape, dim):
    return lax.broadcasted_iota(jnp.int32, shape, dim)


def _row_tiles(ref, n_sub):
    ts = ref.shape[0] // n_sub
    return [slice(u * ts, (u + 1) * ts) for u in range(n_sub)]


def _resident(arr):
    return pl.BlockSpec(arr.shape, lambda *_: (0,) * arr.ndim, pipeline_mode=pl.Buffered(1))


def _mem_kv_kernel(mem_ref, g_ref, w_ref, o_ref):
    mn = _rms(mem_ref[...], g_ref[...])
    o_ref[...] = _dot(mn, w_ref[...]).astype(o_ref.dtype)


def _mem_kv(mem, mem_norm, w_kv):
    b, m, d = mem.shape
    nl = w_kv.shape[0]
    return pl.pallas_call(
        _mem_kv_kernel,
        grid=(nl, b),
        in_specs=[pl.BlockSpec((None, m, d), lambda l, i: (i, 0, 0)),
                  pl.BlockSpec((1, d), lambda l, i: (0, 0)),
                  pl.BlockSpec((None, d, 2 * d), lambda l, i: (l, 0, 0))],
        out_specs=pl.BlockSpec((None, None, m, 2 * d), lambda l, i: (l, i, 0, 0)),
        out_shape=jax.ShapeDtypeStruct((nl, b, m, 2 * d), BF16),
        compiler_params=_cparams(("arbitrary", "arbitrary")),
        name="mem_kv",
    )(mem, mem_norm.reshape(1, d), w_kv.astype(BF16))


def _rwkv_proj_kernel(x_ref, xp_ref, vec_ref, ones_ref, wrkv_ref, w1_ref, w2_ref, a1_ref, a2_ref,
                      g1_ref, g2_ref,
                      r_ref, lw_ref, k_ref, v_ref, kk_ref, b_ref, g_ref, bonus_ref, *, n_sub):
    i = pl.program_id(1)
    vec = vec_ref[...]
    row = lambda n: vec[n:n + 1, :]
    gn, w0, a0, k_k, k_a, r_k = row(0), row(7), row(8), row(9), row(10), row(11)
    ones_bd = ones_ref[...]
    ts = x_ref.shape[0] // n_sub
    subs = range(n_sub)
    rsl = _row_tiles(x_ref, n_sub)
    h = [_rms(x_ref[rsl[u], :], gn) for u in subs]
    hp = _rms(xp_ref[...], gn)[SUBLANES - 1:SUBLANES, :]
    hp = jnp.where(i == 0, 0.0, hp)
    first = _iota((ts, h[0].shape[1]), 0) == 0
    lasts = [hp] + [h[u][ts - 1:ts, :] for u in subs[:-1]]
    dx = [jnp.where(first, lasts[u], pltpu.roll(h[u], 1, axis=0)) - h[u] for u in subs]
    hb = [h[u].astype(BF16) for u in subs]
    dxb = [dx[u].astype(BF16) for u in subs]
    mix = lambda u, j: hb[u] + dxb[u] * row(1 + j).astype(BF16)

    def post_steps(u, raw):
        st = {}

        def s0():
            lw_ref[rsl[u], :] = -math.exp(-0.5) * _sigmoid(w0 + raw["wl"])
            st["a"] = _sigmoid(a0 + raw["al"])
            g_ref[rsl[u], :] = raw["g"].astype(g_ref.dtype)

        def s1():
            kk_raw = raw["k"] * k_k
            kk = kk_raw * lax.rsqrt(jnp.maximum(_group_sum(kk_raw * kk_raw, ones_bd), 1e-24))
            kk_ref[rsl[u], :] = kk.astype(kk_ref.dtype)
            b_ref[rsl[u], :] = (kk * st["a"]).astype(b_ref.dtype)

        def s2():
            k2 = raw["k"] * (1.0 + (st["a"] - 1.0) * k_a)
            bonus = _group_sum(raw["r"] * k2 * r_k, ones_bd) * raw["v"]
            k_ref[rsl[u], :] = k2.astype(k_ref.dtype)
            bonus_ref[rsl[u], :] = bonus.astype(bonus_ref.dtype)
            r_ref[rsl[u], :] = raw["r"].astype(r_ref.dtype)
            v_ref[rsl[u], :] = raw["v"].astype(v_ref.dtype)

        return [s0, s1, s2]

    pending = []
    for u in subs:
        step = lambda: pending.pop(0)() if pending else None
        raw = {}
        raw["wl"] = _dot(jnp.tanh(_dot(mix(u, 1), w1_ref[...])), w2_ref[...])
        raw["al"] = _dot(_dot(mix(u, 4), a1_ref[...]), a2_ref[...])
        raw["g"] = _dot(_sigmoid(_dot(mix(u, 5), g1_ref[...])), g2_ref[...])
        raw["k"] = _dot(mix(u, 2), wrkv_ref[1])
        step()
        raw["r"] = _dot(mix(u, 0), wrkv_ref[0])
        step()
        raw["v"] = _dot(mix(u, 3), wrkv_ref[2])
        step()
        pending = post_steps(u, raw)
    for fn in pending:
        fn()


def _rwkv_proj(x, vecs, ones_bd, w_rkv, w1, w2, a1, a2, g1, g2):
    b, s, d = x.shape
    tm, n_sub = TILES["rwkv_proj"]
    tm = min(tm, s)
    tok = pl.BlockSpec((None, tm, d), lambda bi, i: (bi, i, 0))
    prev = pl.BlockSpec((None, SUBLANES, d),
                        lambda bi, i: (bi, jnp.maximum(i * (tm // SUBLANES) - 1, 0), 0))
    weights = (vecs, ones_bd, w_rkv, w1, w2, a1, a2, g1, g2)
    out_dt = (BF16, F32, BF16, BF16, BF16, BF16, BF16, BF16)
    return pl.pallas_call(
        functools.partial(_rwkv_proj_kernel, n_sub=n_sub),
        grid=(b, s // tm),
        in_specs=[tok, prev] + [_resident(w) for w in weights],
        out_specs=[tok] * len(out_dt),
        out_shape=[jax.ShapeDtypeStruct((b, s, d), dt) for dt in out_dt],
        compiler_params=_cparams(("arbitrary", "arbitrary")),
        name="rwkv_proj",
    )(x, x, *weights)


def _wkv_kernel(r_ref, lw_ref, k_ref, v_ref, kk_ref, b_ref, tri_ref, o_ref, st_ref,
                *, n_chunks, n_pairs):
    L, W = WKV_CHUNK, WKV_PAIR
    nh = W // RWKV_HEAD_DIM

    @pl.when(pl.program_id(1) == 0)
    def _():
        st_ref[...] = jnp.zeros_like(st_ref)

    ri = _iota((W, W), 0)
    ci = _iota((W, W), 1)
    strict = ri > ci
    diag = ri == ci
    incl2 = _iota((W, 2 * W), 0) >= (_iota((W, 2 * W), 1) % W)
    lane_head = _iota((L, W), 1) // RWKV_HEAD_DIM
    tri = tri_ref[...]

    def stack(x):
        return jnp.concatenate([jnp.where(lane_head == h, x, 0.0) for h in range(nh)],
                               axis=0).astype(BF16)

    units = [(c, p) for c in range(n_chunks) for p in range(n_pairs)]
    nu = len(units)
    at, rt, vs, bk, bkh, wlast = [], [], [], [], [], []
    for c in range(n_chunks):
        rows = slice(c * L, (c + 1) * L)
        lw = lw_ref[rows, :]
        cum = _dot_exact_lhs(tri, lw)
        cum_last = cum[L - 1:L, :]
        w_inv = jnp.exp(-cum)
        w_tail = jnp.exp(cum_last - cum)
        w_last = jnp.exp(cum_last)
        k = k_ref[rows, :].astype(F32)
        b = b_ref[rows, :].astype(F32)
        rt_c = r_ref[rows, :].astype(F32) * jnp.exp(cum)
        at_c = -kk_ref[rows, :].astype(F32) * jnp.exp(cum - lw)
        v_c = v_ref[rows, :].astype(F32)
        bt_c, kt_c, bh_c, kh_c = b * w_inv, k * w_inv, b * w_tail, k * w_tail
        for p in range(n_pairs):
            ln = slice(p * W, (p + 1) * W)
            rt.append(stack(rt_c[:, ln]))
            at.append(stack(at_c[:, ln]))
            vs.append(stack(v_c[:, ln]))
            bk.append(jnp.concatenate([stack(bt_c[:, ln]), stack(kt_c[:, ln])], axis=0))
            bkh.append(jnp.concatenate([stack(bh_c[:, ln]), stack(kh_c[:, ln])], axis=0))
            wlast.append(w_last[:, ln])
    a_all = [_dot_nt(at[u], bk[u]) for u in range(nu)]
    r_all = [_dot_nt(rt[u], bk[u]) for u in range(nu)]
    nil = [jnp.where(strict, a_all[u][:, :W], 0.0) for u in range(nu)]
    a_ak = [jnp.where(strict, a_all[u][:, W:], 0.0).astype(BF16) for u in range(nu)]
    a_r = [jnp.where(incl2, r_all[u], 0.0).astype(BF16) for u in range(nu)]
    av = [_dot(a_ak[u], vs[u]).astype(BF16) for u in range(nu)]
    t = [jnp.where(diag, 1.0, nil[u]) for u in range(nu)]
    pw = [nil[u].astype(BF16) for u in range(nu)]
    pw = [_dot(pw[u], pw[u]).astype(BF16) for u in range(nu)]
    for _ in range(4):
        both = [_dot(pw[u], jnp.concatenate([pw[u], t[u].astype(BF16)], axis=1))
                for u in range(nu)]
        pw = [both[u][:, :W].astype(BF16) for u in range(nu)]
        t = [t[u] + both[u][:, W:] for u in range(nu)]
    t = [t[u] + _dot(pw[u], t[u]) for u in range(nu)]
    pu = [_dot(t[u], jnp.concatenate([at[u], av[u]], axis=1)).astype(BF16)
          for u in range(nu)]
    zero = jnp.zeros((W, W), BF16)
    rhs = [jnp.concatenate([pu[u], jnp.concatenate([zero, vs[u]], axis=1)], axis=0)
           for u in range(nu)]
    qo = [_dot(a_r[u], rhs[u]) for u in range(nu)]
    gh = [_dot_tn(bkh[u], rhs[u]) for u in range(nu)]
    qg_lhs = [jnp.concatenate(
        [(rt[u].astype(F32) + qo[u][:, :W]).astype(BF16),
         (jnp.where(diag, wlast[u], 0.0) + gh[u][:, :W]).astype(BF16)], axis=0)
        for u in range(nu)]
    states = [st_ref[p] for p in range(n_pairs)]
    for u, (c, p) in enumerate(units):
        qg = _dot(qg_lhs[u], states[p])
        o_s = qg[:W] + qo[u][:, W:]
        o_ref[c * L:(c + 1) * L, p * W:(p + 1) * W] = (o_s[:L] + o_s[L:]).astype(o_ref.dtype)
        states[p] = qg[W:] + gh[u][:, W:]
    for p in range(n_pairs):
        st_ref[p] = states[p]


def _wkv(r, lw, k, v, kk, b):
    bsz, s, d = r.shape
    tc = min(TILES["wkv"][0], s)
    n_chunks = tc // WKV_CHUNK
    n_pairs = d // WKV_PAIR
    tri = (jnp.arange(WKV_CHUNK)[:, None] >= jnp.arange(WKV_CHUNK)[None, :]).astype(BF16)
    blk = pl.BlockSpec((None, tc, d), lambda bi, i: (bi, i, 0))
    return pl.pallas_call(
        functools.partial(_wkv_kernel, n_chunks=n_chunks, n_pairs=n_pairs),
        grid=(bsz, s // tc),
        in_specs=[blk] * 6 + [pl.BlockSpec(tri.shape, lambda bi, i: (0, 0))],
        out_specs=blk,
        out_shape=jax.ShapeDtypeStruct((bsz, s, d), BF16),
        scratch_shapes=[pltpu.VMEM((n_pairs, WKV_PAIR, WKV_PAIR), F32)],
        compiler_params=_cparams(("arbitrary", "arbitrary")),
        name="wkv",
    )(r, lw, k, v, kk, b, tri)


def _rwkv_out_math(x, o, g, bonus, vec, ones_bd, w_o):
    lnx_g, lnx_b = vec[0:1, :], vec[1:2, :]
    o = o.astype(F32)
    inv_n = 1.0 / RWKV_HEAD_DIM
    mean = _group_sum(o, ones_bd) * inv_n
    dlt = o - mean
    var = _group_sum(dlt * dlt, ones_bd) * inv_n
    y = dlt * lax.rsqrt(var + RWKV_GN_EPS) * lnx_g + lnx_b + bonus.astype(F32)
    return x + _dot(y * g.astype(F32), w_o)


def _xattn_math(xs, g, wq_ref, kv_ref, wo_ref):
    d = xs[0].shape[-1]
    hd = d // XATTN_HEADS
    units = [(u, h) for u in range(len(xs)) for h in range(XATTN_HEADS)]
    q = [_dot(_rms(x, g), wq_ref[...]) for x in xs]
    s = [_dot_nt(q[u][:, h * hd:(h + 1) * hd], kv_ref[:, h * hd:(h + 1) * hd]) * (hd ** -0.5)
         for u, h in units]
    p = [jnp.exp(si - jnp.max(si, axis=-1, keepdims=True)) for si in s]
    p = [pi * (1.0 / jnp.sum(pi, axis=-1, keepdims=True)) for pi in p]
    o = [_dot(p[i], kv_ref[:, d + h * hd:d + (h + 1) * hd]).astype(BF16)
         for i, (u, h) in enumerate(units)]
    return [x + _dot(jnp.concatenate(o[u * XATTN_HEADS:(u + 1) * XATTN_HEADS], axis=-1),
                     wo_ref[...]) for u, x in enumerate(xs)]


def _xattn_kernel(x_ref, g_ref, wq_ref, kv_ref, wo_ref, out_ref, *, n_sub):
    rsl = _row_tiles(x_ref, n_sub)
    ys = _xattn_math([x_ref[r, :] for r in rsl], g_ref[...], wq_ref, kv_ref, wo_ref)
    for r, y in zip(rsl, ys):
        out_ref[r, :] = y


def _rwkv_out_xattn_kernel(x_ref, o_ref, g_ref, bonus_ref, vec_ref, ones_ref, wo1_ref,
                           gx_ref, wq_ref, kv_ref, wo_ref, out_ref, *, n_sub):
    rsl = _row_tiles(x_ref, n_sub)
    xs = [_rwkv_out_math(x_ref[r, :], o_ref[r, :], g_ref[r, :], bonus_ref[r, :], vec_ref[...],
                         ones_ref[...], wo1_ref[...]) for r in rsl]
    ys = _xattn_math(xs, gx_ref[...], wq_ref, kv_ref, wo_ref)
    for r, y in zip(rsl, ys):
        out_ref[r, :] = y


def _rwkv_out_xattn(x, o, g, bonus, vecs, ones_bd, w_o1, gx, w_q, kv, w_o):
    b, s, d = x.shape
    tm, n_sub = TILES["rwkv_out_xattn"]
    tm = min(tm, s)
    m = kv.shape[1]
    tok = pl.BlockSpec((None, tm, d), lambda bi, i: (bi, i, 0))
    res = _resident
    return pl.pallas_call(
        functools.partial(_rwkv_out_xattn_kernel, n_sub=n_sub),
        grid=(b, s // tm),
        in_specs=[tok, tok, tok, tok, res(vecs), res(ones_bd), res(w_o1), res(gx), res(w_q),
                  pl.BlockSpec((None, m, 2 * d), lambda bi, i: (bi, 0, 0)), res(w_o)],
        out_specs=tok,
        out_shape=jax.ShapeDtypeStruct((b, s, d), F32),
        compiler_params=_cparams(("arbitrary", "arbitrary")),
        name="rwkv_out_xattn",
    )(x, o, g, bonus, vecs, ones_bd, w_o1, gx, w_q, kv, w_o)


def _xattn(x, g, w_q, kv, w_o):
    b, s, d = x.shape
    tm, n_sub = TILES["xattn"]
    tm = min(tm, s)
    m = kv.shape[1]
    tok = pl.BlockSpec((None, tm, d), lambda bi, i: (bi, i, 0))
    full = lambda arr: pl.BlockSpec(arr.shape, lambda bi, i: (0,) * arr.ndim)
    return pl.pallas_call(
        functools.partial(_xattn_kernel, n_sub=n_sub),
        grid=(b, s // tm),
        in_specs=[tok, full(g), full(w_q),
                  pl.BlockSpec((None, m, 2 * d), lambda bi, i: (bi, 0, 0)), full(w_o)],
        out_specs=tok,
        out_shape=jax.ShapeDtypeStruct((b, s, d), F32),
        compiler_params=_cparams(("arbitrary", "arbitrary")),
        name="xattn",
    )(x, g, w_q, kv, w_o)


def _ffn_kernel(x_ref, g_ref, wup_ref, cw_ref, cb_ref, wd_ref, gf_ref, out_ref,
                act_sc, carry_sc, *, tiles_per_seq, final_norm, n_sub):
    f = wd_ref.shape[0]
    fc = MXU_TILE
    rsl = _row_tiles(x_ref, n_sub)
    ts = x_ref.shape[0] // n_sub

    @pl.when((pl.program_id(0) % tiles_per_seq) == 0)
    def _():
        carry_sc[...] = jnp.zeros_like(carry_sc)

    xn = [_rms(x_ref[r, :], g_ref[...]).astype(BF16) for r in rsl]
    rows = _iota((ts, fc), 0)
    for j in range(f // fc):
        cols = slice(j * fc, (j + 1) * fc)
        prev = carry_sc[:, cols]
        for u, r in enumerate(rsl):
            gate = jnp.dot(xn[u], wup_ref[:, cols], preferred_element_type=F32)
            val = jnp.dot(xn[u], wup_ref[:, f + j * fc:f + (j + 1) * fc],
                          preferred_element_type=F32)
            s1 = jnp.where(rows == 0, prev[SUBLANES - 1:SUBLANES, :],
                           pltpu.roll(gate, 1, axis=0))
            s2 = pltpu.roll(gate, 2, axis=0)
            s2 = jnp.where(rows == 0, prev[SUBLANES - 2:SUBLANES - 1, :], s2)
            s2 = jnp.where(rows == 1, prev[SUBLANES - 1:SUBLANES, :], s2)
            conv = (s2 * cw_ref[0:1, cols] + s1 * cw_ref[1:2, cols] + gate * cw_ref[2:3, cols]
                    + cb_ref[:, cols])
            act_sc[r, cols] = (conv * _sigmoid(conv) * val).astype(BF16)
            prev = gate[ts - SUBLANES:ts, :]
        carry_sc[:, cols] = prev
    for r in rsl:
        y = x_ref[r, :] + jnp.dot(act_sc[r, :], wd_ref[...], preferred_element_type=F32)
        out_ref[r, :] = _rms(y, gf_ref[...]) if final_norm else y


def _ffn(x, g, w_up, conv_w, conv_b, w_down, g_final, final_norm):
    b, s, d = x.shape
    f = w_down.shape[0]
    tm, n_sub = TILES["ffn"]
    tm = min(tm, s)
    xf = x.reshape(b * s, d)
    tok = pl.BlockSpec((tm, d), lambda t: (t, 0))
    weights = (g, w_up, conv_w, conv_b, w_down, g_final)
    out = pl.pallas_call(
        functools.partial(_ffn_kernel, tiles_per_seq=s // tm, final_norm=final_norm,
                          n_sub=n_sub),
        grid=(b * s // tm,),
        in_specs=[tok] + [_resident(w) for w in weights],
        out_specs=tok,
        out_shape=jax.ShapeDtypeStruct((b * s, d), F32),
        scratch_shapes=[pltpu.VMEM((tm, f), BF16), pltpu.VMEM((SUBLANES, f), F32)],
        compiler_params=_cparams(("arbitrary",)),
        name="ffn",
    )(xf, *weights)
    return out.reshape(b, s, d)


def _sgu_kernel(x_ref, vec_ref, win_ref, ws_ref, bst_ref, expand_ref, wout_ref, out_ref,
                wc_sc, bias_sc, *, n_sub):
    d = x_ref.shape[-1]
    C, G = SGU_CHUNK, SGU_GROUPS
    gd = d // G
    gpb = MXU_TILE // gd
    nblk = d // MXU_TILE

    @pl.when((pl.program_id(0) == 0) & (pl.program_id(1) == 0))
    def _():
        causal = _iota((C, C), 0) >= _iota((C, C), 1)
        for g in range(G):
            wc_sc[g // gpb, :, (g % gpb) * C:(g % gpb + 1) * C] = (
                jnp.where(causal, ws_ref[g], 0.0).astype(BF16))
        bias_sc[...] = _dot_exact_rhs(bst_ref[...], expand_ref[...])

    vec = vec_ref[...]
    gn, ln_g, ln_b = vec[0:1, :], vec[1:2, :], vec[2:3, :]
    lane_grp = _iota((C, MXU_TILE), 1) // gd
    bias = bias_sc[...]
    rsl = _row_tiles(x_ref, n_sub)
    ts = x_ref.shape[0] // n_sub
    subs = range(n_sub)
    x = [x_ref[rsl[s], :] for s in subs]
    z = [jax.nn.gelu(_dot(_rms(x[s], gn), win_ref[...])) for s in subs]
    vn = []
    for s in subs:
        v = z[s][:, d:]
        dv = v - jnp.mean(v, axis=-1, keepdims=True)
        var = jnp.mean(dv * dv, axis=-1, keepdims=True)
        vn.append((dv * lax.rsqrt(var + SGU_LN_EPS) * ln_g + ln_b).astype(BF16))
    gated = []
    for s in subs:
        chunks = []
        for c in range(ts // C):
            blocks = []
            for blk in range(nblk):
                vb = vn[s][c * C:(c + 1) * C, blk * MXU_TILE:(blk + 1) * MXU_TILE]
                rhs = jnp.concatenate(
                    [jnp.where(lane_grp == gi, vb, jnp.zeros_like(vb)) for gi in range(gpb)],
                    axis=0)
                blocks.append(jnp.dot(wc_sc[blk], rhs, preferred_element_type=F32))
            chunks.append(jnp.concatenate(blocks, axis=-1) + bias)
        gated.append((z[s][:, :d] * jnp.concatenate(chunks, axis=0)).astype(BF16))
    for s in subs:
        out_ref[rsl[s], :] = x[s] + jnp.dot(gated[s], wout_ref[...], preferred_element_type=F32)


def _sgu(x, vecs, w_in, w_s, b_s, w_out):
    b, s, d = x.shape
    tm, n_sub = TILES["sgu"]
    tm = min(tm, s)
    gd = d // SGU_GROUPS
    gpb = MXU_TILE // gd
    expand = (jnp.arange(d)[None, :] // gd == jnp.arange(SGU_CHUNK)[:, None]).astype(BF16)
    bst = jnp.pad(b_s.T, ((0, 0), (0, SGU_CHUNK - SGU_GROUPS)))
    tok = pl.BlockSpec((None, tm, d), lambda bi, i: (bi, i, 0))
    full = lambda arr: pl.BlockSpec(arr.shape, lambda bi, i: (0,) * arr.ndim)
    ins = (vecs, w_in, w_s, bst, expand, w_out)
    return pl.pallas_call(
        functools.partial(_sgu_kernel, n_sub=n_sub),
        grid=(b, s // tm),
        in_specs=[tok] + [full(a) for a in ins],
        out_specs=tok,
        out_shape=jax.ShapeDtypeStruct((b, s, d), F32),
        scratch_shapes=[pltpu.VMEM((SGU_GROUPS // gpb, SGU_CHUNK, gpb * SGU_CHUNK), BF16),
                        pltpu.VMEM((SGU_CHUNK, d), F32)],
        compiler_params=_cparams(("arbitrary", "arbitrary")),
        name="sgu",
    )(x, *ins)


def _pad_rows(rows, d):
    n = -(-len(rows) // SUBLANES) * SUBLANES
    arr = jnp.stack([r.reshape(d).astype(F32) for r in rows])
    return jnp.pad(arr, ((0, n - len(rows)), (0, 0)))


def kernel(x, mem, norm_mix, norm_mem, norm_ffn, norm_final, mem_norm, rw_mu, rw_w_rkv, rw_w0, rw_w1, rw_w2, rw_a0, rw_a1, rw_a2, rw_g1, rw_g2, rw_k_k, rw_k_a, rw_r_k, rw_lnx_g, rw_lnx_b, rw_w_o, gm_w_in, gm_ln_g, gm_ln_b, gm_w_s, gm_b_s, gm_w_out, ca_w_q, ca_w_kv, ca_w_o, ff_w_up, ff_conv_w, ff_conv_b, ff_w_down):
    d = x.shape[-1]
    depth = norm_mix.shape[0]
    bf = lambda w: w.astype(BF16)
    lane = jnp.arange(MXU_TILE)
    ones_bd = (lane[:, None] // RWKV_HEAD_DIM == lane[None, :] // RWKV_HEAD_DIM).astype(BF16)
    kv = _mem_kv(mem, mem_norm, ca_w_kv)
    for i in range(depth):
        j = i // 2
        if i % 2 == 0:
            vecs = _pad_rows([norm_mix[i]] + [rw_mu[j, n] for n in range(6)]
                             + [rw_w0[j], rw_a0[j], rw_k_k[j], rw_k_a[j], rw_r_k[j]], d)
            r, lw, k, v, kk, b, g, bonus = _rwkv_proj(
                x, vecs, ones_bd, bf(rw_w_rkv[j]), bf(rw_w1[j]), bf(rw_w2[j]), bf(rw_a1[j]),
                bf(rw_a2[j]), bf(rw_g1[j]), bf(rw_g2[j]))
            o = _wkv(r, lw, k, v, kk, b)
            x = _rwkv_out_xattn(x, o, g, bonus, _pad_rows([rw_lnx_g[j], rw_lnx_b[j]], d),
                                ones_bd, bf(rw_w_o[j]), norm_mem[i].reshape(1, d),
                                bf(ca_w_q[i]), kv[i], bf(ca_w_o[i]))
        else:
            x = _sgu(x, _pad_rows([norm_mix[i], gm_ln_g[j], gm_ln_b[j]], d), bf(gm_w_in[j]),
                     gm_w_s[j], gm_b_s[j], bf(gm_w_out[j]))
            x = _xattn(x, norm_mem[i].reshape(1, d), bf(ca_w_q[i]), kv[i], bf(ca_w_o[i]))
        x = _ffn(x, norm_ffn[i].reshape(1, d), bf(ff_w_up[i]), ff_conv_w[i],
                 ff_conv_b[i].reshape(1, -1), bf(ff_w_down[i]), norm_final.reshape(1, d),
                 final_norm=(i == depth - 1))
    return x
```

```python
import functools
import math

import jax
import jax.numpy as jnp
from jax import lax
from jax.experimental import pallas as pl
from jax.experimental.pallas import tpu as pltpu

F32 = jnp.float32
BF16 = jnp.bfloat16

RWKV_HEAD_DIM = 64
RWKV_GN_EPS = 64e-5
SGU_CHUNK = 128
SGU_GROUPS = 16
SGU_LN_EPS = 1e-5
XATTN_HEADS = 4
RMS_EPS = 1e-6

MXU_TILE = 256
SUBLANES = 8
VMEM_BYTES = 64 * 1024 * 1024
VMEM_LIMIT = VMEM_BYTES - VMEM_BYTES // 8
WKV_CHUNK = 64
WKV_PAIR = 2 * RWKV_HEAD_DIM

TILES = {
    "rwkv_proj": (512, 2),
    "wkv": (256, 1),
    "rwkv_out_xattn": (1024, 1),
    "xattn": (1024, 1),
    "sgu": (1024, 2),
    "ffn": (1024, 2),
}


def _cparams(sem):
    return pltpu.CompilerParams(dimension_semantics=sem, vmem_limit_bytes=VMEM_LIMIT)


def _dot(a, b):
    return jnp.dot(a.astype(BF16), b.astype(BF16), preferred_element_type=F32)


def _dot_nt(a, b):
    return lax.dot_general(a.astype(BF16), b.astype(BF16), (((1,), (1,)), ((), ())),
                           preferred_element_type=F32)


def _dot_tn(a, b):
    return lax.dot_general(a.astype(BF16), b.astype(BF16), (((0,), (0,)), ((), ())),
                           preferred_element_type=F32)


def _split3(x):
    hi = x.astype(BF16)
    r1 = x - hi.astype(F32)
    mid = r1.astype(BF16)
    lo = (r1 - mid.astype(F32)).astype(BF16)
    return hi, mid, lo


def _dot_exact_rhs(x, b_bf16):
    return sum(jnp.dot(t, b_bf16, preferred_element_type=F32) for t in _split3(x))


def _rms(x, g):
    return x * lax.rsqrt(jnp.mean(x * x, axis=-1, keepdims=True) + RMS_EPS) * g


def _sigmoid(x):
    return 1.0 / (1.0 + jnp.exp(-x))


def _group_sum(x, ones_bd):
    d = x.shape[-1]
    parts = [_dot(x[:, j:j + MXU_TILE], ones_bd) for j in range(0, d, MXU_TILE)]
    return jnp.concatenate(parts, axis=-1)


def _iota(shape, dim):
    return lax.broadcasted_iota(jnp.int32, shape, dim)


def _row_tiles(ref, n_sub):
    ts = ref.shape[0] // n_sub
    return [slice(u * ts, (u + 1) * ts) for u in range(n_sub)]


def _resident(arr):
    return pl.BlockSpec(arr.shape, lambda *_: (0,) * arr.ndim, pipeline_mode=pl.Buffered(1))


def _mem_kv_kernel(mem_ref, g_ref, w_ref, o_ref):
    mn = _rms(mem_ref[...], g_ref[...])
    o_ref[...] = _dot(mn, w_ref[...]).astype(o_ref.dtype)


def _mem_kv(mem, mem_norm, w_kv):
    b, m, d = mem.shape
    nl = w_kv.shape[0]
    return pl.pallas_call(
        _mem_kv_kernel,
        grid=(nl, b),
        in_specs=[pl.BlockSpec((None, m, d), lambda l, i: (i, 0, 0)),
                  pl.BlockSpec((1, d), lambda l, i: (0, 0)),
                  pl.BlockSpec((None, d, 2 * d), lambda l, i: (l, 0, 0))],
        out_specs=pl.BlockSpec((None, None, m, 2 * d), lambda l, i: (l, i, 0, 0)),
        out_shape=jax.ShapeDtypeStruct((nl, b, m, 2 * d), BF16),
        compiler_params=_cparams(("arbitrary", "arbitrary")),
        name="mem_kv",
    )(mem, mem_norm.reshape(1, d), w_kv.astype(BF16))


def _rwkv_proj_kernel(x_ref, xp_ref, vec_ref, ones_ref, wrkv_ref, w1_ref, w2_ref, a1_ref, a2_ref,
                      g1_ref, g2_ref,
                      r_ref, lw_ref, k_ref, v_ref, kk_ref, b_ref, g_ref, bonus_ref, *, n_sub):
    i = pl.program_id(1)
    vec = vec_ref[...]
    row = lambda n: vec[n:n + 1, :]
    gn, w0, a0, k_k, k_a, r_k = row(0), row(7), row(8), row(9), row(10), row(11)
    ones_bd = ones_ref[...]
    ts = x_ref.shape[0] // n_sub
    subs = range(n_sub)
    rsl = _row_tiles(x_ref, n_sub)
    h = [_rms(x_ref[rsl[u], :], gn) for u in subs]
    hp = _rms(xp_ref[...], gn)[SUBLANES - 1:SUBLANES, :]
    hp = jnp.where(i == 0, 0.0, hp)
    first = _iota((ts, h[0].shape[1]), 0) == 0
    lasts = [hp] + [h[u][ts - 1:ts, :] for u in subs[:-1]]
    dx = [jnp.where(first, lasts[u], pltpu.roll(h[u], 1, axis=0)) - h[u] for u in subs]
    hb = [h[u].astype(BF16) for u in subs]
    dxb = [dx[u].astype(BF16) for u in subs]
    mix = lambda u, j: hb[u] + dxb[u] * row(1 + j).astype(BF16)

    def post_steps(u, raw):
        st = {}

        def s0():
            lw_ref[rsl[u], :] = -math.exp(-0.5) * _sigmoid(w0 + raw["wl"])
            st["a"] = _sigmoid(a0 + raw["al"])
            g_ref[rsl[u], :] = raw["g"].astype(g_ref.dtype)

        def s1():
            kk_raw = raw["k"] * k_k
            kk = kk_raw * lax.rsqrt(jnp.maximum(_group_sum(kk_raw * kk_raw, ones_bd), 1e-24))
            kk_ref[rsl[u], :] = kk.astype(kk_ref.dtype)
            b_ref[rsl[u], :] = (kk * st["a"]).astype(b_ref.dtype)

        def s2():
            k2 = raw["k"] * (1.0 + (st["a"] - 1.0) * k_a)
            bonus = _group_sum(raw["r"] * k2 * r_k, ones_bd) * raw["v"]
            k_ref[rsl[u], :] = k2.astype(k_ref.dtype)
            bonus_ref[rsl[u], :] = bonus.astype(bonus_ref.dtype)
            r_ref[rsl[u], :] = raw["r"].astype(r_ref.dtype)
            v_ref[rsl[u], :] = raw["v"].astype(v_ref.dtype)

        return [s0, s1, s2]

    pending = []
    for u in subs:
        step = lambda: pending.pop(0)() if pending else None
        raw = {}
        raw["wl"] = _dot(jnp.tanh(_dot(mix(u, 1), w1_ref[...])), w2_ref[...])
        raw["al"] = _dot(_dot(mix(u, 4), a1_ref[...]), a2_ref[...])
        raw["g"] = _dot(_sigmoid(_dot(mix(u, 5), g1_ref[...])), g2_ref[...])
        raw["k"] = _dot(mix(u, 2), wrkv_ref[1])
        step()
        raw["r"] = _dot(mix(u, 0), wrkv_ref[0])
        step()
        raw["v"] = _dot(mix(u, 3), wrkv_ref[2])
        step()
        pending = post_steps(u, raw)
    for fn in pending:
        fn()


def _rwkv_proj(x, vecs, ones_bd, w_rkv, w1, w2, a1, a2, g1, g2):
    b, s, d = x.shape
    tm, n_sub = TILES["rwkv_proj"]
    tm = min(tm, s)
    tok = pl.BlockSpec((None, tm, d), lambda bi, i: (bi, i, 0))
    prev = pl.BlockSpec((None, SUBLANES, d),
                        lambda bi, i: (bi, jnp.maximum(i * (tm // SUBLANES) - 1, 0), 0))
    weights = (vecs, ones_bd, w_rkv, w1, w2, a1, a2, g1, g2)
    out_dt = (BF16, F32, BF16, BF16, BF16, BF16, BF16, BF16)
    return pl.pallas_call(
        functools.partial(_rwkv_proj_kernel, n_sub=n_sub),
        grid=(b, s // tm),
        in_specs=[tok, prev] + [_resident(w) for w in weights],
        out_specs=[tok] * len(out_dt),
        out_shape=[jax.ShapeDtypeStruct((b, s, d), dt) for dt in out_dt],
        compiler_params=_cparams(("arbitrary", "arbitrary")),
        name="rwkv_proj",
    )(x, x, *weights)


def _wkv_kernel(r_ref, lw_ref, k_ref, v_ref, kk_ref, b_ref, o_ref, st_ref,
                *, n_chunks, n_pairs):
    L, W = WKV_CHUNK, WKV_PAIR
    nh = W // RWKV_HEAD_DIM

    @pl.when(pl.program_id(1) == 0)
    def _():
        st_ref[...] = jnp.zeros_like(st_ref)

    ri = _iota((W, W), 0)
    ci = _iota((W, W), 1)
    strict = ri > ci
    diag = ri == ci
    incl2 = _iota((W, 2 * W), 0) >= (_iota((W, 2 * W), 1) % W)
    lane_head = _iota((L, W), 1) // RWKV_HEAD_DIM
    row_id = _iota((L, r_ref.shape[1]), 0)

    def stack(x):
        return jnp.concatenate([jnp.where(lane_head == h, x, 0.0) for h in range(nh)],
                               axis=0).astype(BF16)

    units = [(c, p) for c in range(n_chunks) for p in range(n_pairs)]
    nu = len(units)
    at, rt, vs, bk, bkh, wlast = [], [], [], [], [], []
    for c in range(n_chunks):
        rows = slice(c * L, (c + 1) * L)
        lw = lw_ref[rows, :]
        cum = lw
        shift = 1
        while shift < L:
            cum = cum + jnp.where(row_id >= shift, pltpu.roll(cum, shift, axis=0), 0.0)
            shift *= 2
        cum_last = cum[L - 1:L, :]
        w_inv = jnp.exp(-cum)
        w_tail = jnp.exp(cum_last - cum)
        w_last = jnp.exp(cum_last)
        k = k_ref[rows, :].astype(F32)
        b = b_ref[rows, :].astype(F32)
        rt_c = r_ref[rows, :].astype(F32) * jnp.exp(cum)
        at_c = -kk_ref[rows, :].astype(F32) * jnp.exp(cum - lw)
        v_c = v_ref[rows, :].astype(F32)
        bt_c, kt_c, bh_c, kh_c = b * w_inv, k * w_inv, b * w_tail, k * w_tail
        for p in range(n_pairs):
            ln = slice(p * W, (p + 1) * W)
            rt.append(stack(rt_c[:, ln]))
            at.append(stack(at_c[:, ln]))
            vs.append(stack(v_c[:, ln]))
            bk.append(jnp.concatenate([stack(bt_c[:, ln]), stack(kt_c[:, ln])], axis=0))
            bkh.append(jnp.concatenate([stack(bh_c[:, ln]), stack(kh_c[:, ln])], axis=0))
            wlast.append(w_last[:, ln])
    a_all = [_dot_nt(at[u], bk[u]) for u in range(nu)]
    r_all = [_dot_nt(rt[u], bk[u]) for u in range(nu)]
    nil = [jnp.where(strict, a_all[u][:, :W], 0.0) for u in range(nu)]
    a_ak = [jnp.where(strict, a_all[u][:, W:], 0.0).astype(BF16) for u in range(nu)]
    a_r = [jnp.where(incl2, r_all[u], 0.0).astype(BF16) for u in range(nu)]
    av = [_dot(a_ak[u], vs[u]).astype(BF16) for u in range(nu)]
    t = [jnp.where(diag, 1.0, nil[u]) for u in range(nu)]
    pw = [nil[u].astype(BF16) for u in range(nu)]
    pw = [_dot(pw[u], pw[u]).astype(BF16) for u in range(nu)]
    for _ in range(4):
        both = [_dot(pw[u], jnp.concatenate([pw[u], t[u].astype(BF16)], axis=1))
                for u in range(nu)]
        pw = [both[u][:, :W].astype(BF16) for u in range(nu)]
        t = [t[u] + both[u][:, W:] for u in range(nu)]
    t = [t[u] + _dot(pw[u], t[u]) for u in range(nu)]
    pu = [_dot(t[u], jnp.concatenate([at[u], av[u]], axis=1)).astype(BF16)
          for u in range(nu)]
    zero = jnp.zeros((W, W), BF16)
    rhs = [jnp.concatenate([pu[u], jnp.concatenate([zero, vs[u]], axis=1)], axis=0)
           for u in range(nu)]
    qo = [_dot(a_r[u], rhs[u]) for u in range(nu)]
    gh = [_dot_tn(bkh[u], rhs[u]) for u in range(nu)]
    qg_lhs = [jnp.concatenate(
        [(rt[u].astype(F32) + qo[u][:, :W]).astype(BF16),
         (jnp.where(diag, wlast[u], 0.0) + gh[u][:, :W]).astype(BF16)], axis=0)
        for u in range(nu)]
    states = [st_ref[p] for p in range(n_pairs)]
    for u, (c, p) in enumerate(units):
        qg = _dot(qg_lhs[u], states[p])
        o_s = qg[:W] + qo[u][:, W:]
        o_ref[c * L:(c + 1) * L, p * W:(p + 1) * W] = (o_s[:L] + o_s[L:]).astype(o_ref.dtype)
        states[p] = qg[W:] + gh[u][:, W:]
    for p in range(n_pairs):
        st_ref[p] = states[p]


def _wkv(r, lw, k, v, kk, b):
    bsz, s, d = r.shape
    tc = min(TILES["wkv"][0], s)
    n_chunks = tc // WKV_CHUNK
    n_pairs = d // WKV_PAIR
    blk = pl.BlockSpec((None, tc, d), lambda bi, i: (bi, i, 0))
    return pl.pallas_call(
        functools.partial(_wkv_kernel, n_chunks=n_chunks, n_pairs=n_pairs),
        grid=(bsz, s // tc),
        in_specs=[blk] * 6,
        out_specs=blk,
        out_shape=jax.ShapeDtypeStruct((bsz, s, d), BF16),
        scratch_shapes=[pltpu.VMEM((n_pairs, WKV_PAIR, WKV_PAIR), F32)],
        compiler_params=_cparams(("arbitrary", "arbitrary")),
        name="wkv",
    )(r, lw, k, v, kk, b)


def _rwkv_out_math(x, o, g, bonus, vec, ones_bd, w_o):
    lnx_g, lnx_b = vec[0:1, :], vec[1:2, :]
    o = o.astype(F32)
    inv_n = 1.0 / RWKV_HEAD_DIM
    mean = _group_sum(o, ones_bd) * inv_n
    dlt = o - mean
    var = _group_sum(dlt * dlt, ones_bd) * inv_n
    y = dlt * lax.rsqrt(var + RWKV_GN_EPS) * lnx_g + lnx_b + bonus.astype(F32)
    return x + _dot(y * g.astype(F32), w_o)


def _xattn_math(xs, g, wq_ref, kv_ref, wo_ref):
    d = xs[0].shape[-1]
    hd = d // XATTN_HEADS
    units = [(u, h) for u in range(len(xs)) for h in range(XATTN_HEADS)]
    q = [_dot(_rms(x, g), wq_ref[...]) for x in xs]
    s = [_dot_nt(q[u][:, h * hd:(h + 1) * hd], kv_ref[:, h * hd:(h + 1) * hd]) * (hd ** -0.5)
         for u, h in units]
    p = [jnp.exp(si - jnp.max(si, axis=-1, keepdims=True)) for si in s]
    p = [pi * (1.0 / jnp.sum(pi, axis=-1, keepdims=True)) for pi in p]
    o = [_dot(p[i], kv_ref[:, d + h * hd:d + (h + 1) * hd]).astype(BF16)
         for i, (u, h) in enumerate(units)]
    return [x + _dot(jnp.concatenate(o[u * XATTN_HEADS:(u + 1) * XATTN_HEADS], axis=-1),
                     wo_ref[...]) for u, x in enumerate(xs)]


def _xattn_kernel(x_ref, g_ref, wq_ref, kv_ref, wo_ref, out_ref, *, n_sub):
    rsl = _row_tiles(x_ref, n_sub)
    ys = _xattn_math([x_ref[r, :] for r in rsl], g_ref[...], wq_ref, kv_ref, wo_ref)
    for r, y in zip(rsl, ys):
        out_ref[r, :] = y


def _rwkv_out_xattn_kernel(x_ref, o_ref, g_ref, bonus_ref, vec_ref, ones_ref, wo1_ref,
                           gx_ref, wq_ref, kv_ref, wo_ref, out_ref, *, n_sub):
    rsl = _row_tiles(x_ref, n_sub)
    xs = [_rwkv_out_math(x_ref[r, :], o_ref[r, :], g_ref[r, :], bonus_ref[r, :], vec_ref[...],
                         ones_ref[...], wo1_ref[...]) for r in rsl]
    ys = _xattn_math(xs, gx_ref[...], wq_ref, kv_ref, wo_ref)
    for r, y in zip(rsl, ys):
        out_ref[r, :] = y


def _rwkv_out_xattn(x, o, g, bonus, vecs, ones_bd, w_o1, gx, w_q, kv, w_o):
    b, s, d = x.shape
    tm, n_sub = TILES["rwkv_out_xattn"]
    tm = min(tm, s)
    m = kv.shape[1]
    tok = pl.BlockSpec((None, tm, d), lambda bi, i: (bi, i, 0))
    res = _resident
    return pl.pallas_call(
        functools.partial(_rwkv_out_xattn_kernel, n_sub=n_sub),
        grid=(b, s // tm),
        in_specs=[tok, tok, tok, tok, res(vecs), res(ones_bd), res(w_o1), res(gx), res(w_q),
                  pl.BlockSpec((None, m, 2 * d), lambda bi, i: (bi, 0, 0)), res(w_o)],
        out_specs=tok,
        out_shape=jax.ShapeDtypeStruct((b, s, d), F32),
        compiler_params=_cparams(("arbitrary", "arbitrary")),
        name="rwkv_out_xattn",
    )(x, o, g, bonus, vecs, ones_bd, w_o1, gx, w_q, kv, w_o)


def _xattn(x, g, w_q, kv, w_o):
    b, s, d = x.shape
    tm, n_sub = TILES["xattn"]
    tm = min(tm, s)
    m = kv.shape[1]
    tok = pl.BlockSpec((None, tm, d), lambda bi, i: (bi, i, 0))
    full = lambda arr: pl.BlockSpec(arr.shape, lambda bi, i: (0,) * arr.ndim)
    return pl.pallas_call(
        functools.partial(_xattn_kernel, n_sub=n_sub),
        grid=(b, s // tm),
        in_specs=[tok, full(g), full(w_q),
                  pl.BlockSpec((None, m, 2 * d), lambda bi, i: (bi, 0, 0)), full(w_o)],
        out_specs=tok,
        out_shape=jax.ShapeDtypeStruct((b, s, d), F32),
        compiler_params=_cparams(("arbitrary", "arbitrary")),
        name="xattn",
    )(x, g, w_q, kv, w_o)


def _ffn_kernel(x_ref, g_ref, wup_ref, cw_ref, cb_ref, wd_ref, gf_ref, out_ref,
                act_sc, carry_sc, *, tiles_per_seq, final_norm, n_sub):
    f = wd_ref.shape[0]
    fc = MXU_TILE
    rsl = _row_tiles(x_ref, n_sub)
    ts = x_ref.shape[0] // n_sub

    @pl.when((pl.program_id(0) % tiles_per_seq) == 0)
    def _():
        carry_sc[...] = jnp.zeros_like(carry_sc)

    xn = [_rms(x_ref[r, :], g_ref[...]).astype(BF16) for r in rsl]
    rows = _iota((ts, fc), 0)
    for j in range(f // fc):
        cols = slice(j * fc, (j + 1) * fc)
        prev = carry_sc[:, cols]
        for u, r in enumerate(rsl):
            gate = jnp.dot(xn[u], wup_ref[:, cols], preferred_element_type=F32)
            val = jnp.dot(xn[u], wup_ref[:, f + j * fc:f + (j + 1) * fc],
                          preferred_element_type=F32)
            s1 = jnp.where(rows == 0, prev[SUBLANES - 1:SUBLANES, :],
                           pltpu.roll(gate, 1, axis=0))
            s2 = pltpu.roll(gate, 2, axis=0)
            s2 = jnp.where(rows == 0, prev[SUBLANES - 2:SUBLANES - 1, :], s2)
            s2 = jnp.where(rows == 1, prev[SUBLANES - 1:SUBLANES, :], s2)
            conv = (s2 * cw_ref[0:1, cols] + s1 * cw_ref[1:2, cols] + gate * cw_ref[2:3, cols]
                    + cb_ref[:, cols])
            act_sc[r, cols] = (conv * _sigmoid(conv) * val).astype(BF16)
            prev = gate[ts - SUBLANES:ts, :]
        carry_sc[:, cols] = prev
    for r in rsl:
        y = x_ref[r, :] + jnp.dot(act_sc[r, :], wd_ref[...], preferred_element_type=F32)
        out_ref[r, :] = _rms(y, gf_ref[...]) if final_norm else y


def _ffn(x, g, w_up, conv_w, conv_b, w_down, g_final, final_norm):
    b, s, d = x.shape
    f = w_down.shape[0]
    tm, n_sub = TILES["ffn"]
    tm = min(tm, s)
    xf = x.reshape(b * s, d)
    tok = pl.BlockSpec((tm, d), lambda t: (t, 0))
    weights = (g, w_up, conv_w, conv_b, w_down, g_final)
    out = pl.pallas_call(
        functools.partial(_ffn_kernel, tiles_per_seq=s // tm, final_norm=final_norm,
                          n_sub=n_sub),
        grid=(b * s // tm,),
        in_specs=[tok] + [_resident(w) for w in weights],
        out_specs=tok,
        out_shape=jax.ShapeDtypeStruct((b * s, d), F32),
        scratch_shapes=[pltpu.VMEM((tm, f), BF16), pltpu.VMEM((SUBLANES, f), F32)],
        compiler_params=_cparams(("arbitrary",)),
        name="ffn",
    )(xf, *weights)
    return out.reshape(b, s, d)


def _sgu_kernel(x_ref, vec_ref, win_ref, ws_ref, bst_ref, expand_ref, wout_ref, out_ref,
                wc_sc, bias_sc, *, n_sub):
    d = x_ref.shape[-1]
    C, G = SGU_CHUNK, SGU_GROUPS
    gd = d // G
    gpb = MXU_TILE // gd
    nblk = d // MXU_TILE

    @pl.when((pl.program_id(0) == 0) & (pl.program_id(1) == 0))
    def _():
        causal = _iota((C, C), 0) >= _iota((C, C), 1)
        for g in range(G):
            wc_sc[g // gpb, :, (g % gpb) * C:(g % gpb + 1) * C] = (
                jnp.where(causal, ws_ref[g], 0.0).astype(BF16))
        bias_sc[...] = _dot_exact_rhs(bst_ref[...], expand_ref[...])

    vec = vec_ref[...]
    gn, ln_g, ln_b = vec[0:1, :], vec[1:2, :], vec[2:3, :]
    lane_grp = _iota((C, MXU_TILE), 1) // gd
    bias = bias_sc[...]
    rsl = _row_tiles(x_ref, n_sub)
    ts = x_ref.shape[0] // n_sub
    subs = range(n_sub)
    x = [x_ref[rsl[s], :] for s in subs]
    z = [jax.nn.gelu(_dot(_rms(x[s], gn), win_ref[...])) for s in subs]
    vn = []
    for s in subs:
        v = z[s][:, d:]
        dv = v - jnp.mean(v, axis=-1, keepdims=True)
        var = jnp.mean(dv * dv, axis=-1, keepdims=True)
        vn.append((dv * lax.rsqrt(var + SGU_LN_EPS) * ln_g + ln_b).astype(BF16))
    gated = []
    for s in subs:
        chunks = []
        for c in range(ts // C):
            blocks = []
            for blk in range(nblk):
                vb = vn[s][c * C:(c + 1) * C, blk * MXU_TILE:(blk + 1) * MXU_TILE]
                rhs = jnp.concatenate(
                    [jnp.where(lane_grp == gi, vb, jnp.zeros_like(vb)) for gi in range(gpb)],
                    axis=0)
                blocks.append(jnp.dot(wc_sc[blk], rhs, preferred_element_type=F32))
            chunks.append(jnp.concatenate(blocks, axis=-1) + bias)
        gated.append((z[s][:, :d] * jnp.concatenate(chunks, axis=0)).astype(BF16))
    for s in subs:
        out_ref[rsl[s], :] = x[s] + jnp.dot(gated[s], wout_ref[...], preferred_element_type=F32)


def _sgu(x, vecs, w_in, w_s, b_s, w_out):
    b, s, d = x.shape
    tm, n_sub = TILES["sgu"]
    tm = min(tm, s)
    gd = d // SGU_GROUPS
    gpb = MXU_TILE // gd
    expand = (jnp.arange(d)[None, :] // gd == jnp.arange(SGU_CHUNK)[:, None]).astype(BF16)
    bst = jnp.pad(b_s.T, ((0, 0), (0, SGU_CHUNK - SGU_GROUPS)))
    tok = pl.BlockSpec((None, tm, d), lambda bi, i: (bi, i, 0))
    full = lambda arr: pl.BlockSpec(arr.shape, lambda bi, i: (0,) * arr.ndim)
    ins = (vecs, w_in, w_s, bst, expand, w_out)
    return pl.pallas_call(
        functools.partial(_sgu_kernel, n_sub=n_sub),
        grid=(b, s // tm),
        in_specs=[tok] + [full(a) for a in ins],
        out_specs=tok,
        out_shape=jax.ShapeDtypeStruct((b, s, d), F32),
        scratch_shapes=[pltpu.VMEM((SGU_GROUPS // gpb, SGU_CHUNK, gpb * SGU_CHUNK), BF16),
                        pltpu.VMEM((SGU_CHUNK, d), F32)],
        compiler_params=_cparams(("arbitrary", "arbitrary")),
        name="sgu",
    )(x, *ins)


def _pad_rows(rows, d):
    n = -(-len(rows) // SUBLANES) * SUBLANES
    arr = jnp.stack([r.reshape(d).astype(F32) for r in rows])
    return jnp.pad(arr, ((0, n - len(rows)), (0, 0)))


def kernel(x, mem, norm_mix, norm_mem, norm_ffn, norm_final, mem_norm, rw_mu, rw_w_rkv, rw_w0, rw_w1, rw_w2, rw_a0, rw_a1, rw_a2, rw_g1, rw_g2, rw_k_k, rw_k_a, rw_r_k, rw_lnx_g, rw_lnx_b, rw_w_o, gm_w_in, gm_ln_g, gm_ln_b, gm_w_s, gm_b_s, gm_w_out, ca_w_q, ca_w_kv, ca_w_o, ff_w_up, ff_conv_w, ff_conv_b, ff_w_down):
    d = x.shape[-1]
    depth = norm_mix.shape[0]
    bf = lambda w: w.astype(BF16)
    lane = jnp.arange(MXU_TILE)
    ones_bd = (lane[:, None] // RWKV_HEAD_DIM == lane[None, :] // RWKV_HEAD_DIM).astype(BF16)
    kv = _mem_kv(mem, mem_norm, ca_w_kv)
    for i in range(depth):
        j = i // 2
        if i % 2 == 0:
            vecs = _pad_rows([norm_mix[i]] + [rw_mu[j, n] for n in range(6)]
                             + [rw_w0[j], rw_a0[j], rw_k_k[j], rw_k_a[j], rw_r_k[j]], d)
            r, lw, k, v, kk, b, g, bonus = _rwkv_proj(
                x, vecs, ones_bd, bf(rw_w_rkv[j]), bf(rw_w1[j]), bf(rw_w2[j]), bf(rw_a1[j]),
                bf(rw_a2[j]), bf(rw_g1[j]), bf(rw_g2[j]))
            o = _wkv(r, lw, k, v, kk, b)
            x = _rwkv_out_xattn(x, o, g, bonus, _pad_rows([rw_lnx_g[j], rw_lnx_b[j]], d),
                                ones_bd, bf(rw_w_o[j]), norm_mem[i].reshape(1, d),
                                bf(ca_w_q[i]), kv[i], bf(ca_w_o[i]))
        else:
            x = _sgu(x, _pad_rows([norm_mix[i], gm_ln_g[j], gm_ln_b[j]], d), bf(gm_w_in[j]),
                     gm_w_s[j], gm_b_s[j], bf(gm_w_out[j]))
            x = _xattn(x, norm_mem[i].reshape(1, d), bf(ca_w_q[i]), kv[i], bf(ca_w_o[i]))
        x = _ffn(x, norm_ffn[i].reshape(1, d), bf(ff_w_up[i]), ff_conv_w[i],
                 ff_conv_b[i].reshape(1, -1), bf(ff_w_down[i]), norm_final.reshape(1, d),
                 final_norm=(i == depth - 1))
    return x
```

```python
import functools
import math

import jax
import jax.numpy as jnp
from jax import lax
from jax.experimental import pallas as pl
from jax.experimental.pallas import tpu as pltpu

F32 = jnp.float32
BF16 = jnp.bfloat16

RWKV_HEAD_DIM = 64
RWKV_GN_EPS = 64e-5
SGU_CHUNK = 128
SGU_GROUPS = 16
SGU_LN_EPS = 1e-5
XATTN_HEADS = 4
RMS_EPS = 1e-6

MXU_TILE = 256
SUBLANES = 8
VMEM_BYTES = 64 * 1024 * 1024
VMEM_LIMIT = VMEM_BYTES - VMEM_BYTES // 8
WKV_CHUNK = 64
WKV_PAIR = 2 * RWKV_HEAD_DIM

TILES = {
    "rwkv_proj": (512, 2),
    "wkv": (256, 1),
    "rwkv_out_xattn": (1024, 1),
    "xattn": (1024, 1),
    "sgu": (1024, 2),
    "ffn": (1024, 2),
}


def _cparams(sem):
    return pltpu.CompilerParams(dimension_semantics=sem, vmem_limit_bytes=VMEM_LIMIT)


def _dot(a, b):
    return jnp.dot(a.astype(BF16), b.astype(BF16), preferred_element_type=F32)


def _dot_nt(a, b):
    return lax.dot_general(a.astype(BF16), b.astype(BF16), (((1,), (1,)), ((), ())),
                           preferred_element_type=F32)


def _dot_tn(a, b):
    return lax.dot_general(a.astype(BF16), b.astype(BF16), (((0,), (0,)), ((), ())),
                           preferred_element_type=F32)


def _split3(x):
    hi = x.astype(BF16)
    r1 = x - hi.astype(F32)
    mid = r1.astype(BF16)
    lo = (r1 - mid.astype(F32)).astype(BF16)
    return hi, mid, lo


def _dot_exact_rhs(x, b_bf16):
    return sum(jnp.dot(t, b_bf16, preferred_element_type=F32) for t in _split3(x))


def _rms(x, g):
    return x * lax.rsqrt(jnp.mean(x * x, axis=-1, keepdims=True) + RMS_EPS) * g


def _sigmoid(x):
    return 1.0 / (1.0 + jnp.exp(-x))


def _group_sum(x, ones_bd):
    d = x.shape[-1]
    parts = [_dot(x[:, j:j + MXU_TILE], ones_bd) for j in range(0, d, MXU_TILE)]
    return jnp.concatenate(parts, axis=-1)


def _iota(shape, dim):
    return lax.broadcasted_iota(jnp.int32, shape, dim)


def _row_tiles(ref, n_sub):
    ts = ref.shape[0] // n_sub
    return [slice(u * ts, (u + 1) * ts) for u in range(n_sub)]


def _resident(arr):
    return pl.BlockSpec(arr.shape, lambda *_: (0,) * arr.ndim, pipeline_mode=pl.Buffered(1))


def _mem_kv_kernel(mem_ref, g_ref, w_ref, o_ref):
    mn = _rms(mem_ref[...], g_ref[...])
    o_ref[...] = _dot(mn, w_ref[...]).astype(o_ref.dtype)


def _mem_kv(mem, mem_norm, w_kv):
    b, m, d = mem.shape
    nl = w_kv.shape[0]
    return pl.pallas_call(
        _mem_kv_kernel,
        grid=(nl, b),
        in_specs=[pl.BlockSpec((None, m, d), lambda l, i: (i, 0, 0)),
                  pl.BlockSpec((1, d), lambda l, i: (0, 0)),
                  pl.BlockSpec((None, d, 2 * d), lambda l, i: (l, 0, 0))],
        out_specs=pl.BlockSpec((None, None, m, 2 * d), lambda l, i: (l, i, 0, 0)),
        out_shape=jax.ShapeDtypeStruct((nl, b, m, 2 * d), BF16),
        compiler_params=_cparams(("arbitrary", "arbitrary")),
        name="mem_kv",
    )(mem, mem_norm.reshape(1, d), w_kv.astype(BF16))


def _rwkv_proj_kernel(x_ref, xp_ref, vec_ref, ones_ref, wrkv_ref, w1_ref, w2_ref, a1_ref, a2_ref,
                      g1_ref, g2_ref,
                      r_ref, lw_ref, k_ref, v_ref, kk_ref, b_ref, g_ref, bonus_ref, *, n_sub):
    i = pl.program_id(1)
    vec = vec_ref[...]
    row = lambda n: vec[n:n + 1, :]
    gn, w0, a0, k_k, k_a, r_k = row(0), row(7), row(8), row(9), row(10), row(11)
    ones_bd = ones_ref[...]
    ts = x_ref.shape[0] // n_sub
    subs = range(n_sub)
    rsl = _row_tiles(x_ref, n_sub)
    h = [_rms(x_ref[rsl[u], :], gn) for u in subs]
    hp = _rms(xp_ref[...], gn)[SUBLANES - 1:SUBLANES, :]
    hp = jnp.where(i == 0, 0.0, hp)
    first = _iota((ts, h[0].shape[1]), 0) == 0
    lasts = [hp] + [h[u][ts - 1:ts, :] for u in subs[:-1]]
    dx = [jnp.where(first, lasts[u], pltpu.roll(h[u], 1, axis=0)) - h[u] for u in subs]
    hb = [h[u].astype(BF16) for u in subs]
    dxb = [dx[u].astype(BF16) for u in subs]
    mix = lambda u, j: hb[u] + dxb[u] * row(1 + j).astype(BF16)

    def post_steps(u, raw):
        st = {}

        def s0():
            lw_ref[rsl[u], :] = -math.exp(-0.5) * _sigmoid(w0 + raw["wl"])
            st["a"] = _sigmoid(a0 + raw["al"])
            g_ref[rsl[u], :] = raw["g"].astype(g_ref.dtype)

        def s1():
            kk_raw = raw["k"] * k_k
            kk = kk_raw * lax.rsqrt(jnp.maximum(_group_sum(kk_raw * kk_raw, ones_bd), 1e-24))
            kk_ref[rsl[u], :] = kk.astype(kk_ref.dtype)
            b_ref[rsl[u], :] = (kk * st["a"]).astype(b_ref.dtype)

        def s2():
            k2 = raw["k"] * (1.0 + (st["a"] - 1.0) * k_a)
            bonus = _group_sum(raw["r"] * k2 * r_k, ones_bd) * raw["v"]
            k_ref[rsl[u], :] = k2.astype(k_ref.dtype)
            bonus_ref[rsl[u], :] = bonus.astype(bonus_ref.dtype)
            r_ref[rsl[u], :] = raw["r"].astype(r_ref.dtype)
            v_ref[rsl[u], :] = raw["v"].astype(v_ref.dtype)

        return [s0, s1, s2]

    pending = []
    for u in subs:
        step = lambda: pending.pop(0)() if pending else None
        raw = {}
        raw["wl"] = _dot(jnp.tanh(_dot(mix(u, 1), w1_ref[...])), w2_ref[...])
        raw["al"] = _dot(_dot(mix(u, 4), a1_ref[...]), a2_ref[...])
        raw["g"] = _dot(_sigmoid(_dot(mix(u, 5), g1_ref[...])), g2_ref[...])
        raw["k"] = _dot(mix(u, 2), wrkv_ref[1])
        step()
        raw["r"] = _dot(mix(u, 0), wrkv_ref[0])
        step()
        raw["v"] = _dot(mix(u, 3), wrkv_ref[2])
        step()
        pending = post_steps(u, raw)
    for fn in pending:
        fn()


def _rwkv_proj(x, vecs, ones_bd, w_rkv, w1, w2, a1, a2, g1, g2):
    b, s, d = x.shape
    tm, n_sub = TILES["rwkv_proj"]
    tm = min(tm, s)
    tok = pl.BlockSpec((None, tm, d), lambda bi, i: (bi, i, 0))
    prev = pl.BlockSpec((None, SUBLANES, d),
                        lambda bi, i: (bi, jnp.maximum(i * (tm // SUBLANES) - 1, 0), 0))
    weights = (vecs, ones_bd, w_rkv, w1, w2, a1, a2, g1, g2)
    out_dt = (BF16, F32, BF16, BF16, BF16, BF16, BF16, BF16)
    return pl.pallas_call(
        functools.partial(_rwkv_proj_kernel, n_sub=n_sub),
        grid=(b, s // tm),
        in_specs=[tok, prev] + [_resident(w) for w in weights],
        out_specs=[tok] * len(out_dt),
        out_shape=[jax.ShapeDtypeStruct((b, s, d), dt) for dt in out_dt],
        compiler_params=_cparams(("arbitrary", "arbitrary")),
        name="rwkv_proj",
    )(x, x, *weights)


def _wkv_kernel(r_ref, lw_ref, k_ref, v_ref, kk_ref, b_ref, o_ref, st_ref,
                *, n_chunks, n_pairs):
    L, W = WKV_CHUNK, WKV_PAIR
    nh = W // RWKV_HEAD_DIM

    @pl.when(pl.program_id(1) == 0)
    def _():
        st_ref[...] = jnp.zeros_like(st_ref)

    ri = _iota((W, W), 0)
    ci = _iota((W, W), 1)
    strict = ri > ci
    diag = ri == ci
    incl2 = _iota((W, 2 * W), 0) >= (_iota((W, 2 * W), 1) % W)
    lane_head = _iota((L, W), 1) // RWKV_HEAD_DIM
    row_id = _iota((L, r_ref.shape[1]), 0)

    def stack(x):
        return jnp.concatenate([jnp.where(lane_head == h, x, 0.0) for h in range(nh)],
                               axis=0).astype(BF16)

    units = [(c, p) for c in range(n_chunks) for p in range(n_pairs)]
    nu = len(units)
    at, rt, vs, bk, bkh, wlast, a_all, r_all = [], [], [], [], [], [], [], []
    for c in range(n_chunks):
        rows = slice(c * L, (c + 1) * L)
        lw = lw_ref[rows, :]
        cum = lw
        shift = 1
        while shift < L:
            cum = cum + jnp.where(row_id >= shift, pltpu.roll(cum, shift, axis=0), 0.0)
            shift *= 2
        cum_last = cum[L - 1:L, :]
        w_inv = jnp.exp(-cum)
        w_tail = jnp.exp(cum_last - cum)
        w_last = jnp.exp(cum_last)
        k = k_ref[rows, :].astype(F32)
        b = b_ref[rows, :].astype(F32)
        rt_c = r_ref[rows, :].astype(F32) * jnp.exp(cum)
        at_c = -kk_ref[rows, :].astype(F32) * jnp.exp(cum - lw)
        v_c = v_ref[rows, :].astype(F32)
        bt_c, kt_c, bh_c, kh_c = b * w_inv, k * w_inv, b * w_tail, k * w_tail
        for p in range(n_pairs):
            ln = slice(p * W, (p + 1) * W)
            rt.append(stack(rt_c[:, ln]))
            at.append(stack(at_c[:, ln]))
            vs.append(stack(v_c[:, ln]))
            bk.append(jnp.concatenate([stack(bt_c[:, ln]), stack(kt_c[:, ln])], axis=0))
            bkh.append(jnp.concatenate([stack(bh_c[:, ln]), stack(kh_c[:, ln])], axis=0))
            wlast.append(w_last[:, ln])
        for u in range(c * n_pairs, (c + 1) * n_pairs):
            a_all.append(_dot_nt(at[u], bk[u]))
            r_all.append(_dot_nt(rt[u], bk[u]))
    nil = [jnp.where(strict, a_all[u][:, :W], 0.0) for u in range(nu)]
    a_ak = [jnp.where(strict, a_all[u][:, W:], 0.0).astype(BF16) for u in range(nu)]
    a_r = [jnp.where(incl2, r_all[u], 0.0).astype(BF16) for u in range(nu)]
    av = [_dot(a_ak[u], vs[u]).astype(BF16) for u in range(nu)]
    t = [jnp.where(diag, 1.0, nil[u]) for u in range(nu)]
    pw = [nil[u].astype(BF16) for u in range(nu)]
    pw = [_dot(pw[u], pw[u]).astype(BF16) for u in range(nu)]
    for _ in range(4):
        both = [_dot(pw[u], jnp.concatenate([pw[u], t[u].astype(BF16)], axis=1))
                for u in range(nu)]
        pw = [both[u][:, :W].astype(BF16) for u in range(nu)]
        t = [t[u] + both[u][:, W:] for u in range(nu)]
    t = [t[u] + _dot(pw[u], t[u]) for u in range(nu)]
    pu = [_dot(t[u], jnp.concatenate([at[u], av[u]], axis=1)).astype(BF16)
          for u in range(nu)]
    zero = jnp.zeros((W, W), BF16)
    rhs = [jnp.concatenate([pu[u], jnp.concatenate([zero, vs[u]], axis=1)], axis=0)
           for u in range(nu)]
    qo = [_dot(a_r[u], rhs[u]) for u in range(nu)]
    gh = [_dot_tn(bkh[u], rhs[u]) for u in range(nu)]
    qg_lhs = [jnp.concatenate(
        [(rt[u].astype(F32) + qo[u][:, :W]).astype(BF16),
         (jnp.where(diag, wlast[u], 0.0) + gh[u][:, :W]).astype(BF16)], axis=0)
        for u in range(nu)]
    states = [st_ref[p] for p in range(n_pairs)]
    for u, (c, p) in enumerate(units):
        qg = _dot(qg_lhs[u], states[p])
        o_s = qg[:W] + qo[u][:, W:]
        o_ref[c * L:(c + 1) * L, p * W:(p + 1) * W] = (o_s[:L] + o_s[L:]).astype(o_ref.dtype)
        states[p] = qg[W:] + gh[u][:, W:]
    for p in range(n_pairs):
        st_ref[p] = states[p]


def _wkv(r, lw, k, v, kk, b):
    bsz, s, d = r.shape
    tc = min(TILES["wkv"][0], s)
    n_chunks = tc // WKV_CHUNK
    n_pairs = d // WKV_PAIR
    blk = pl.BlockSpec((None, tc, d), lambda bi, i: (bi, i, 0))
    return pl.pallas_call(
        functools.partial(_wkv_kernel, n_chunks=n_chunks, n_pairs=n_pairs),
        grid=(bsz, s // tc),
        in_specs=[blk] * 6,
        out_specs=blk,
        out_shape=jax.ShapeDtypeStruct((bsz, s, d), BF16),
        scratch_shapes=[pltpu.VMEM((n_pairs, WKV_PAIR, WKV_PAIR), F32)],
        compiler_params=_cparams(("arbitrary", "arbitrary")),
        name="wkv",
    )(r, lw, k, v, kk, b)


def _rwkv_out_math(x, o, g, bonus, vec, ones_bd, w_o):
    lnx_g, lnx_b = vec[0:1, :], vec[1:2, :]
    o = o.astype(F32)
    inv_n = 1.0 / RWKV_HEAD_DIM
    mean = _group_sum(o, ones_bd) * inv_n
    dlt = o - mean
    var = _group_sum(dlt * dlt, ones_bd) * inv_n
    y = dlt * lax.rsqrt(var + RWKV_GN_EPS) * lnx_g + lnx_b + bonus.astype(F32)
    return x + _dot(y * g.astype(F32), w_o)


def _xattn_math(xs, g, wq_ref, kv_ref, wo_ref):
    d = xs[0].shape[-1]
    hd = d // XATTN_HEADS
    units = [(u, h) for u in range(len(xs)) for h in range(XATTN_HEADS)]
    q = [_dot(_rms(x, g), wq_ref[...]) for x in xs]
    s = [_dot_nt(q[u][:, h * hd:(h + 1) * hd], kv_ref[:, h * hd:(h + 1) * hd]) * (hd ** -0.5)
         for u, h in units]
    p = [jnp.exp(si - jnp.max(si, axis=-1, keepdims=True)) for si in s]
    p = [pi * (1.0 / jnp.sum(pi, axis=-1, keepdims=True)) for pi in p]
    o = [_dot(p[i], kv_ref[:, d + h * hd:d + (h + 1) * hd]).astype(BF16)
         for i, (u, h) in enumerate(units)]
    return [x + _dot(jnp.concatenate(o[u * XATTN_HEADS:(u + 1) * XATTN_HEADS], axis=-1),
                     wo_ref[...]) for u, x in enumerate(xs)]


def _xattn_kernel(x_ref, g_ref, wq_ref, kv_ref, wo_ref, out_ref, *, n_sub):
    rsl = _row_tiles(x_ref, n_sub)
    ys = _xattn_math([x_ref[r, :] for r in rsl], g_ref[...], wq_ref, kv_ref, wo_ref)
    for r, y in zip(rsl, ys):
        out_ref[r, :] = y


def _rwkv_out_xattn_kernel(x_ref, o_ref, g_ref, bonus_ref, vec_ref, ones_ref, wo1_ref,
                           gx_ref, wq_ref, kv_ref, wo_ref, out_ref, *, n_sub):
    rsl = _row_tiles(x_ref, n_sub)
    xs = [_rwkv_out_math(x_ref[r, :], o_ref[r, :], g_ref[r, :], bonus_ref[r, :], vec_ref[...],
                         ones_ref[...], wo1_ref[...]) for r in rsl]
    ys = _xattn_math(xs, gx_ref[...], wq_ref, kv_ref, wo_ref)
    for r, y in zip(rsl, ys):
        out_ref[r, :] = y


def _rwkv_out_xattn(x, o, g, bonus, vecs, ones_bd, w_o1, gx, w_q, kv, w_o):
    b, s, d = x.shape
    tm, n_sub = TILES["rwkv_out_xattn"]
    tm = min(tm, s)
    m = kv.shape[1]
    tok = pl.BlockSpec((None, tm, d), lambda bi, i: (bi, i, 0))
    res = _resident
    return pl.pallas_call(
        functools.partial(_rwkv_out_xattn_kernel, n_sub=n_sub),
        grid=(b, s // tm),
        in_specs=[tok, tok, tok, tok, res(vecs), res(ones_bd), res(w_o1), res(gx), res(w_q),
                  pl.BlockSpec((None, m, 2 * d), lambda bi, i: (bi, 0, 0)), res(w_o)],
        out_specs=tok,
        out_shape=jax.ShapeDtypeStruct((b, s, d), F32),
        compiler_params=_cparams(("arbitrary", "arbitrary")),
        name="rwkv_out_xattn",
    )(x, o, g, bonus, vecs, ones_bd, w_o1, gx, w_q, kv, w_o)


def _xattn(x, g, w_q, kv, w_o):
    b, s, d = x.shape
    tm, n_sub = TILES["xattn"]
    tm = min(tm, s)
    m = kv.shape[1]
    tok = pl.BlockSpec((None, tm, d), lambda bi, i: (bi, i, 0))
    full = lambda arr: pl.BlockSpec(arr.shape, lambda bi, i: (0,) * arr.ndim)
    return pl.pallas_call(
        functools.partial(_xattn_kernel, n_sub=n_sub),
        grid=(b, s // tm),
        in_specs=[tok, full(g), full(w_q),
                  pl.BlockSpec((None, m, 2 * d), lambda bi, i: (bi, 0, 0)), full(w_o)],
        out_specs=tok,
        out_shape=jax.ShapeDtypeStruct((b, s, d), F32),
        compiler_params=_cparams(("arbitrary", "arbitrary")),
        name="xattn",
    )(x, g, w_q, kv, w_o)


def _ffn_kernel(x_ref, g_ref, wup_ref, cw_ref, cb_ref, wd_ref, gf_ref, out_ref,
                act_sc, carry_sc, *, tiles_per_seq, final_norm, n_sub):
    f = wd_ref.shape[0]
    fc = MXU_TILE
    rsl = _row_tiles(x_ref, n_sub)
    ts = x_ref.shape[0] // n_sub

    @pl.when((pl.program_id(0) % tiles_per_seq) == 0)
    def _():
        carry_sc[...] = jnp.zeros_like(carry_sc)

    xn = [_rms(x_ref[r, :], g_ref[...]).astype(BF16) for r in rsl]
    rows = _iota((ts, fc), 0)
    for j in range(f // fc):
        cols = slice(j * fc, (j + 1) * fc)
        prev = carry_sc[:, cols]
        for u, r in enumerate(rsl):
            gate = jnp.dot(xn[u], wup_ref[:, cols], preferred_element_type=F32)
            val = jnp.dot(xn[u], wup_ref[:, f + j * fc:f + (j + 1) * fc],
                          preferred_element_type=F32)
            s1 = jnp.where(rows == 0, prev[SUBLANES - 1:SUBLANES, :],
                           pltpu.roll(gate, 1, axis=0))
            s2 = pltpu.roll(gate, 2, axis=0)
            s2 = jnp.where(rows == 0, prev[SUBLANES - 2:SUBLANES - 1, :], s2)
            s2 = jnp.where(rows == 1, prev[SUBLANES - 1:SUBLANES, :], s2)
            conv = (s2 * cw_ref[0:1, cols] + s1 * cw_ref[1:2, cols] + gate * cw_ref[2:3, cols]
                    + cb_ref[:, cols])
            act_sc[r, cols] = (conv * _sigmoid(conv) * val).astype(BF16)
            prev = gate[ts - SUBLANES:ts, :]
        carry_sc[:, cols] = prev
    for r in rsl:
        y = x_ref[r, :] + jnp.dot(act_sc[r, :], wd_ref[...], preferred_element_type=F32)
        out_ref[r, :] = _rms(y, gf_ref[...]) if final_norm else y


def _ffn(x, g, w_up, conv_w, conv_b, w_down, g_final, final_norm):
    b, s, d = x.shape
    f = w_down.shape[0]
    tm, n_sub = TILES["ffn"]
    tm = min(tm, s)
    xf = x.reshape(b * s, d)
    tok = pl.BlockSpec((tm, d), lambda t: (t, 0))
    weights = (g, w_up, conv_w, conv_b, w_down, g_final)
    out = pl.pallas_call(
        functools.partial(_ffn_kernel, tiles_per_seq=s // tm, final_norm=final_norm,
                          n_sub=n_sub),
        grid=(b * s // tm,),
        in_specs=[tok] + [_resident(w) for w in weights],
        out_specs=tok,
        out_shape=jax.ShapeDtypeStruct((b * s, d), F32),
        scratch_shapes=[pltpu.VMEM((tm, f), BF16), pltpu.VMEM((SUBLANES, f), F32)],
        compiler_params=_cparams(("arbitrary",)),
        name="ffn",
    )(xf, *weights)
    return out.reshape(b, s, d)


def _sgu_kernel(x_ref, vec_ref, win_ref, ws_ref, bst_ref, expand_ref, wout_ref, out_ref,
                wc_sc, bias_sc, *, n_sub):
    d = x_ref.shape[-1]
    C, G = SGU_CHUNK, SGU_GROUPS
    gd = d // G
    gpb = MXU_TILE // gd
    nblk = d // MXU_TILE

    @pl.when((pl.program_id(0) == 0) & (pl.program_id(1) == 0))
    def _():
        causal = _iota((C, C), 0) >= _iota((C, C), 1)
        for g in range(G):
            wc_sc[g // gpb, :, (g % gpb) * C:(g % gpb + 1) * C] = (
                jnp.where(causal, ws_ref[g], 0.0).astype(BF16))
        bias_sc[...] = _dot_exact_rhs(bst_ref[...], expand_ref[...])

    vec = vec_ref[...]
    gn, ln_g, ln_b = vec[0:1, :], vec[1:2, :], vec[2:3, :]
    lane_grp = _iota((C, MXU_TILE), 1) // gd
    bias = bias_sc[...]
    rsl = _row_tiles(x_ref, n_sub)
    ts = x_ref.shape[0] // n_sub
    subs = range(n_sub)
    x = [x_ref[rsl[s], :] for s in subs]
    z = [jax.nn.gelu(_dot(_rms(x[s], gn), win_ref[...])) for s in subs]
    vn = []
    for s in subs:
        v = z[s][:, d:]
        dv = v - jnp.mean(v, axis=-1, keepdims=True)
        var = jnp.mean(dv * dv, axis=-1, keepdims=True)
        vn.append((dv * lax.rsqrt(var + SGU_LN_EPS) * ln_g + ln_b).astype(BF16))
    gated = []
    for s in subs:
        chunks = []
        for c in range(ts // C):
            blocks = []
            for blk in range(nblk):
                vb = vn[s][c * C:(c + 1) * C, blk * MXU_TILE:(blk + 1) * MXU_TILE]
                rhs = jnp.concatenate(
                    [jnp.where(lane_grp == gi, vb, jnp.zeros_like(vb)) for gi in range(gpb)],
                    axis=0)
                blocks.append(jnp.dot(wc_sc[blk], rhs, preferred_element_type=F32))
            chunks.append(jnp.concatenate(blocks, axis=-1) + bias)
        gated.append((z[s][:, :d] * jnp.concatenate(chunks, axis=0)).astype(BF16))
    for s in subs:
        out_ref[rsl[s], :] = x[s] + jnp.dot(gated[s], wout_ref[...], preferred_element_type=F32)


def _sgu(x, vecs, w_in, w_s, b_s, w_out):
    b, s, d = x.shape
    tm, n_sub = TILES["sgu"]
    tm = min(tm, s)
    gd = d // SGU_GROUPS
    gpb = MXU_TILE // gd
    expand = (jnp.arange(d)[None, :] // gd == jnp.arange(SGU_CHUNK)[:, None]).astype(BF16)
    bst = jnp.pad(b_s.T, ((0, 0), (0, SGU_CHUNK - SGU_GROUPS)))
    tok = pl.BlockSpec((None, tm, d), lambda bi, i: (bi, i, 0))
    full = lambda arr: pl.BlockSpec(arr.shape, lambda bi, i: (0,) * arr.ndim)
    ins = (vecs, w_in, w_s, bst, expand, w_out)
    return pl.pallas_call(
        functools.partial(_sgu_kernel, n_sub=n_sub),
        grid=(b, s // tm),
        in_specs=[tok] + [full(a) for a in ins],
        out_specs=tok,
        out_shape=jax.ShapeDtypeStruct((b, s, d), F32),
        scratch_shapes=[pltpu.VMEM((SGU_GROUPS // gpb, SGU_CHUNK, gpb * SGU_CHUNK), BF16),
                        pltpu.VMEM((SGU_CHUNK, d), F32)],
        compiler_params=_cparams(("arbitrary", "arbitrary")),
        name="sgu",
    )(x, *ins)


def _pad_rows(rows, d):
    n = -(-len(rows) // SUBLANES) * SUBLANES
    arr = jnp.stack([r.reshape(d).astype(F32) for r in rows])
    return jnp.pad(arr, ((0, n - len(rows)), (0, 0)))


def kernel(x, mem, norm_mix, norm_mem, norm_ffn, norm_final, mem_norm, rw_mu, rw_w_rkv, rw_w0, rw_w1, rw_w2, rw_a0, rw_a1, rw_a2, rw_g1, rw_g2, rw_k_k, rw_k_a, rw_r_k, rw_lnx_g, rw_lnx_b, rw_w_o, gm_w_in, gm_ln_g, gm_ln_b, gm_w_s, gm_b_s, gm_w_out, ca_w_q, ca_w_kv, ca_w_o, ff_w_up, ff_conv_w, ff_conv_b, ff_w_down):
    d = x.shape[-1]
    depth = norm_mix.shape[0]
    bf = lambda w: w.astype(BF16)
    lane = jnp.arange(MXU_TILE)
    ones_bd = (lane[:, None] // RWKV_HEAD_DIM == lane[None, :] // RWKV_HEAD_DIM).astype(BF16)
    kv = _mem_kv(mem, mem_norm, ca_w_kv)
    for i in range(depth):
        j = i // 2
        if i % 2 == 0:
            vecs = _pad_rows([norm_mix[i]] + [rw_mu[j, n] for n in range(6)]
                             + [rw_w0[j], rw_a0[j], rw_k_k[j], rw_k_a[j], rw_r_k[j]], d)
            r, lw, k, v, kk, b, g, bonus = _rwkv_proj(
                x, vecs, ones_bd, bf(rw_w_rkv[j]), bf(rw_w1[j]), bf(rw_w2[j]), bf(rw_a1[j]),
                bf(rw_a2[j]), bf(rw_g1[j]), bf(rw_g2[j]))
            o = _wkv(r, lw, k, v, kk, b)
            x = _rwkv_out_xattn(x, o, g, bonus, _pad_rows([rw_lnx_g[j], rw_lnx_b[j]], d),
                                ones_bd, bf(rw_w_o[j]), norm_mem[i].reshape(1, d),
                                bf(ca_w_q[i]), kv[i], bf(ca_w_o[i]))
        else:
            x = _sgu(x, _pad_rows([norm_mix[i], gm_ln_g[j], gm_ln_b[j]], d), bf(gm_w_in[j]),
                     gm_w_s[j], gm_b_s[j], bf(gm_w_out[j]))
            x = _xattn(x, norm_mem[i].reshape(1, d), bf(ca_w_q[i]), kv[i], bf(ca_w_o[i]))
        x = _ffn(x, norm_ffn[i].reshape(1, d), bf(ff_w_up[i]), ff_conv_w[i],
                 ff_conv_b[i].reshape(1, -1), bf(ff_w_down[i]), norm_final.reshape(1, d),
                 final_norm=(i == depth - 1))
    return x
```
